```python
import jax, jax.numpy as jnp
from jax import lax
import numpy as np

D_MODEL = 2048
BATCH = 2
SEQ = 4096
DEPTH = 4

CHUNK = 64
N_META = 16
META_PAD = CHUNK - N_META
N_MIXERS = 2
N_RET_LAYERS = (DEPTH + 1) // 2
N_SSD_LAYERS = DEPTH // 2
NORM_EPS = 1e-6

RET_HEADS = 8
RET_DK = D_MODEL // RET_HEADS
RET_DV = 2 * RET_DK
RET_QK = RET_HEADS * RET_DK
RET_VDIM = RET_HEADS * RET_DV
RET_PROJ = 2 * RET_QK + 2 * RET_VDIM
ROPE_BASE = 10000.0

SSD_DI = 2 * D_MODEL
SSD_HEADDIM = 64
SSD_HEADS = SSD_DI // SSD_HEADDIM
SSD_GROUPS = 8
SSD_HPG = SSD_HEADS // SSD_GROUPS
SSD_STATE = 128
SSD_CONV = 4
SSD_BC = SSD_GROUPS * SSD_STATE
SSD_CONV_DIM = SSD_DI + 2 * SSD_BC
SSD_PROJ = 2 * SSD_DI + 2 * SSD_BC + SSD_HEADS

FFN_DIM = 11 * D_MODEL // 4
FFN_CONV = 3

kernel_name = 'hybrid_retention_ssd_convffn_meta'


def rmsnorm(x, w):
    xf = x.astype(jnp.float32)
    y = xf * lax.rsqrt(jnp.mean(xf * xf, axis=-1, keepdims=True) + NORM_EPS)
    return (y * w.astype(jnp.float32)).astype(x.dtype)


def causal_dwconv(x, w, b):
    width, ch = w.shape
    out = lax.conv_general_dilated(
        x, w[:, None, :].astype(x.dtype), window_strides=(1,), padding=[(width - 1, 0)],
        dimension_numbers=('NWC', 'WIO', 'NWC'), feature_group_count=ch)
    return out + b.astype(x.dtype)


def to_chunks(a):
    return jnp.moveaxis(a.reshape(a.shape[0], -1, CHUNK, *a.shape[2:]), 1, 0)


def from_chunks(a):
    a = jnp.moveaxis(a, 0, 1)
    return a.reshape(a.shape[0], -1, *a.shape[3:])


def rotary(x, pos):
    half = x.shape[-1] // 2
    inv = ROPE_BASE ** (-jnp.arange(half, dtype=jnp.float32) / half)
    ang = pos[:, None] * inv[None, :]
    cos = jnp.cos(ang)[None, :, None, :]
    sin = jnp.sin(ang)[None, :, None, :]
    x1, x2 = x[..., :half], x[..., half:]
    return jnp.concatenate([x1 * cos - x2 * sin, x1 * sin + x2 * cos], axis=-1)


def retention_mixer(hn, pos, w_in, gn_w, w_out):
    bsz, length, _ = hn.shape
    f32 = jnp.float32
    proj = hn @ w_in
    q, k, v, g = jnp.split(proj, [RET_QK, 2 * RET_QK, 2 * RET_QK + RET_VDIM], axis=-1)
    q = rotary(q.reshape(bsz, length, RET_HEADS, RET_DK).astype(f32), pos)
    k = rotary(k.reshape(bsz, length, RET_HEADS, RET_DK).astype(f32), pos) * (RET_DK ** -0.5)
    v = v.reshape(bsz, length, RET_HEADS, RET_DV).astype(f32)
    log_gamma = jnp.log1p(-jnp.exp2(-5.0 - jnp.arange(RET_HEADS, dtype=f32)))
    idx = jnp.arange(CHUNK, dtype=f32)
    diff = idx[:, None] - idx[None, :]
    intra_decay = jnp.where(diff[None] >= 0,
                            jnp.exp(jnp.maximum(diff, 0.0)[None] * log_gamma[:, None, None]), 0.0)
    q_decay = jnp.exp((idx + 1.0)[:, None] * log_gamma[None, :])[None, :, :, None]
    k_decay = jnp.exp((CHUNK - 1.0 - idx)[:, None] * log_gamma[None, :])[None, :, :, None]
    chunk_decay = jnp.exp(CHUNK * log_gamma)[None, :, None, None]

    def step(state, inp):
        qc, kc, vc = inp
        scores = jnp.einsum('blhd,bshd->bhls', qc, kc) * intra_decay
        o = jnp.einsum('bhls,bshe->blhe', scores, vc)
        o = o + jnp.einsum('blhd,bhde->blhe', qc * q_decay, state)
        state = state * chunk_decay + jnp.einsum('bshd,bshe->bhde', kc * k_decay, vc)
        return state, o

    s0 = jnp.zeros((bsz, RET_HEADS, RET_DK, RET_DV), f32)
    _, o = lax.scan(step, s0, (to_chunks(q), to_chunks(k), to_chunks(v)))
    o = from_chunks(o)
    o = o * lax.rsqrt(jnp.mean(o * o, axis=-1, keepdims=True) + NORM_EPS)
    o = o.reshape(bsz, length, RET_VDIM) * gn_w.astype(f32)
    y = jax.nn.silu(g.astype(f32)) * o
    return y.astype(hn.dtype) @ w_out


def ssd_mixer(hn, valid, w_in, conv_w, conv_b, dt_bias, a_log, d_skip, gnorm_w, w_out):
    bsz, length, _ = hn.shape
    f32 = jnp.float32
    proj = hn @ w_in
    z, xbc, dt = jnp.split(proj, [SSD_DI, SSD_DI + SSD_CONV_DIM], axis=-1)
    xbc = jax.nn.silu(causal_dwconv(xbc, conv_w, conv_b))
    xs, bm, cm = jnp.split(xbc, [SSD_DI, SSD_DI + SSD_BC], axis=-1)
    xs = (xs * valid[None, :, None]).astype(f32).reshape(bsz, length, SSD_GROUPS, SSD_HPG, SSD_HEADDIM)
    bm = bm.astype(f32).reshape(bsz, length, SSD_GROUPS, SSD_STATE)
    cm = cm.astype(f32).reshape(bsz, length, SSD_GROUPS, SSD_STATE)
    dt = jax.nn.softplus(dt.astype(f32) + dt_bias.astype(f32)).reshape(bsz, length, SSD_GROUPS, SSD_HPG)
    da = dt * (-jnp.exp(a_log.astype(f32))).reshape(SSD_GROUPS, SSD_HPG)
    xdt = xs * dt[..., None]
    causal = jnp.tril(jnp.ones((CHUNK, CHUNK), dtype=bool))[None, :, :, None, None]

    def step(state, inp):
        xc, bc, cc, dac = inp
        cum = jnp.cumsum(dac, axis=1)
        seg = cum[:, :, None] - cum[:, None, :]
        decay = jnp.exp(jnp.where(causal, seg, -jnp.inf))
        attn = jnp.einsum('blgn,bsgn->blsg', cc, bc)[..., None] * decay
        y = jnp.einsum('blsgj,bsgjp->blgjp', attn, xc)
        y = y + jnp.einsum('blgn,bgjpn->blgjp', cc, state) * jnp.exp(cum)[..., None]
        to_end = jnp.exp(cum[:, -1:] - cum)
        state = state * jnp.exp(cum[:, -1])[..., None, None] + \
            jnp.einsum('bsgn,bsgj,bsgjp->bgjpn', bc, to_end, xc)
        return state, y

    s0 = jnp.zeros((bsz, SSD_GROUPS, SSD_HPG, SSD_HEADDIM, SSD_STATE), f32)
    _, ys = lax.scan(step, s0, (to_chunks(xdt), to_chunks(bm), to_chunks(cm), to_chunks(da)))
    y = from_chunks(ys) + xs * d_skip.astype(f32).reshape(SSD_GROUPS, SSD_HPG)[..., None]
    y = y.reshape(bsz, length, SSD_DI) * jax.nn.silu(z.astype(f32))
    yg = y.reshape(bsz, length, SSD_GROUPS, SSD_DI // SSD_GROUPS)
    yg = yg * lax.rsqrt(jnp.mean(yg * yg, axis=-1, keepdims=True) + NORM_EPS)
    y = yg.reshape(bsz, length, SSD_DI) * gnorm_w.astype(f32)
    return y.astype(hn.dtype) @ w_out


def conv_ffn(hn, w_up, conv_w, conv_b, w_down):
    gate, up = jnp.split(hn @ w_up, 2, axis=-1)
    gate = causal_dwconv(gate, conv_w, conv_b)
    return (jax.nn.silu(gate) * up) @ w_down


def setup_inputs(seed: int = 0) -> dict:
    key = jax.random.key(seed)
    ks = jax.random.split(key, 24)
    nrm = jax.random.normal
    f32 = jnp.float32
    D = D_MODEL
    dt0 = jnp.exp(jax.random.uniform(ks[10], (N_SSD_LAYERS, SSD_HEADS), f32)
                  * (np.log(0.1) - np.log(0.001)) + np.log(0.001)).astype(f32)
    return {
        'x': nrm(ks[0], (BATCH, SEQ, D), f32),
        'meta_tokens': nrm(ks[1], (N_META, D), f32),
        'ret_norm_w': 1.0 + 0.02 * nrm(ks[2], (N_RET_LAYERS, D), f32),
        'ret_w_in': nrm(ks[3], (N_RET_LAYERS, D, RET_PROJ), f32) * D ** -0.5,
        'ret_gn_w': 1.0 + 0.02 * nrm(ks[4], (N_RET_LAYERS, RET_VDIM), f32),
        'ret_w_out': nrm(ks[5], (N_RET_LAYERS, RET_VDIM, D), f32) * RET_VDIM ** -0.5,
        'ssd_norm_w': 1.0 + 0.02 * nrm(ks[6], (N_SSD_LAYERS, D), f32),
        'ssd_w_in': nrm(ks[7], (N_SSD_LAYERS, D, SSD_PROJ), f32) * D ** -0.5,
        'ssd_conv_w': nrm(ks[8], (N_SSD_LAYERS, SSD_CONV, SSD_CONV_DIM), f32) * SSD_CONV ** -0.5,
        'ssd_conv_b': 0.02 * nrm(ks[9], (N_SSD_LAYERS, SSD_CONV_DIM), f32),
        'ssd_dt_bias': dt0 + jnp.log(-jnp.expm1(-dt0)),
        'ssd_a_log': jnp.log(jax.random.uniform(ks[11], (N_SSD_LAYERS, SSD_HEADS), f32, 1.0, 16.0)),
        'ssd_d': 1.0 + 0.02 * nrm(ks[12], (N_SSD_LAYERS, SSD_HEADS), f32),
        'ssd_gnorm_w': 1.0 + 0.02 * nrm(ks[13], (N_SSD_LAYERS, SSD_DI), f32),
        'ssd_w_out': nrm(ks[14], (N_SSD_LAYERS, SSD_DI, D), f32) * SSD_DI ** -0.5,
        'ffn_norm_w': 1.0 + 0.02 * nrm(ks[15], (DEPTH, D), f32),
        'ffn_w_up': nrm(ks[16], (DEPTH, D, 2 * FFN_DIM), f32) * D ** -0.5,
        'ffn_conv_w': nrm(ks[17], (DEPTH, FFN_CONV, FFN_DIM), f32) * FFN_CONV ** -0.5,
        'ffn_conv_b': 0.02 * nrm(ks[18], (DEPTH, FFN_DIM), f32),
        'ffn_w_down': nrm(ks[19], (DEPTH, FFN_DIM, D), f32) * FFN_DIM ** -0.5,
        'final_norm_w': 1.0 + 0.02 * nrm(ks[20], (D,), f32),
    }


def reference(x, meta_tokens, ret_norm_w, ret_w_in, ret_gn_w, ret_w_out,
              ssd_norm_w, ssd_w_in, ssd_conv_w, ssd_conv_b, ssd_dt_bias, ssd_a_log, ssd_d,
              ssd_gnorm_w, ssd_w_out, ffn_norm_w, ffn_w_up, ffn_conv_w, ffn_conv_b, ffn_w_down,
              final_norm_w):
    bsz, seq, d = x.shape
    dtype = x.dtype
    length = META_PAD + N_META + seq
    meta = jnp.broadcast_to(meta_tokens.astype(dtype)[None], (bsz, N_META, d))
    h = jnp.concatenate([jnp.zeros((bsz, META_PAD, d), dtype), meta, x], axis=1)
    pos_i = jnp.arange(length) - META_PAD
    valid = (pos_i >= 0).astype(dtype)
    vmask = valid[None, :, None]
    pos = pos_i.astype(jnp.float32)
    for i in range(DEPTH):
        j = i // N_MIXERS
        if i % N_MIXERS == 0:
            mix = retention_mixer(rmsnorm(h, ret_norm_w[j]), pos, ret_w_in[j], ret_gn_w[j], ret_w_out[j])
        else:
            mix = ssd_mixer(rmsnorm(h, ssd_norm_w[j]), valid, ssd_w_in[j], ssd_conv_w[j], ssd_conv_b[j],
                            ssd_dt_bias[j], ssd_a_log[j], ssd_d[j], ssd_gnorm_w[j], ssd_w_out[j])
        h = (h + mix) * vmask
        ffn = conv_ffn(rmsnorm(h, ffn_norm_w[i]), ffn_w_up[i], ffn_conv_w[i], ffn_conv_b[i], ffn_w_down[i])
        h = (h + ffn) * vmask
    h = rmsnorm(h, final_norm_w)
    return h[:, META_PAD + N_META:]
```

```python
import functools

import jax
import jax.numpy as jnp
import numpy as np
from jax import lax
from jax.experimental import pallas as pl
from jax.experimental.pallas import tpu as pltpu

F32 = jnp.float32
BF16 = jnp.bfloat16

D_MODEL = 2048
CHUNK = 64
N_META = 16
META_PAD = CHUNK - N_META
NORM_EPS = 1e-6

RET_HEADS = 8
RET_DK = 256
RET_DV = 512
RET_QK = RET_HEADS * RET_DK
RET_VDIM = RET_HEADS * RET_DV
ROPE_BASE = 10000.0

SSD_DI = 4096
SSD_HEADDIM = 64
SSD_HEADS = 64
SSD_GROUPS = 8
SSD_HPG = 8
SSD_STATE = 128
SSD_BC = SSD_GROUPS * SSD_STATE
SSD_CONV_DIM = SSD_DI + 2 * SSD_BC
SSD_GW = SSD_HPG * SSD_HEADDIM

FFN_DIM = 5632

V7X_VMEM_BYTES = 64 * 1024 * 1024
VMEM_LIMIT = V7X_VMEM_BYTES - 8 * 1024 * 1024

ROW_TILE = 832
COL_TILE = 512
K_TILE = 512
CARRY = 8


def _params(sem):
    return pltpu.CompilerParams(dimension_semantics=sem, vmem_limit_bytes=VMEM_LIMIT)


def _sigmoid(x):
    return 1.0 / (1.0 + jnp.exp(-x))


def _silu(x):
    return x * _sigmoid(x)


def _softplus(x):
    return jnp.maximum(x, 0.0) + jnp.log1p(jnp.exp(-jnp.abs(x)))


def _dot(a, b):
    return jnp.dot(a, b, preferred_element_type=F32)


def _dot_nt(a, b):
    return lax.dot_general(a, b, (((1,), (1,)), ((), ())), preferred_element_type=F32)


def _dot_tn(a, b):
    return lax.dot_general(a, b, (((0,), (0,)), ((), ())), preferred_element_type=F32)


def _split3(x):
    hi = x.astype(BF16)
    r1 = x - hi.astype(F32)
    mid = r1.astype(BF16)
    lo = (r1 - mid.astype(F32)).astype(BF16)
    return hi, mid, lo


def _dot01(sel, x, sel_first):
    parts = _split3(x)
    if sel_first:
        return _dot(sel, parts[0]) + _dot(sel, parts[1]) + _dot(sel, parts[2])
    return _dot(parts[0], sel) + _dot(parts[1], sel) + _dot(parts[2], sel)


def _rms_scale(x, w):
    ms = jnp.mean(x * x, axis=-1, keepdims=True)
    return x * lax.rsqrt(ms + NORM_EPS) * w


def _cast_weight(w_ref, wb_ref):
    @pl.when(pl.program_id(1) == 0)
    def _():
        wb_ref[...] = w_ref[...].astype(BF16)


def _rmsnorm_kernel(x_ref, w_ref, o_ref):
    o_ref[...] = _rms_scale(x_ref[...], w_ref[...]).astype(o_ref.dtype)


def _rmsnorm(x, w):
    rows, d = x.shape
    return pl.pallas_call(
        _rmsnorm_kernel,
        grid=(rows // ROW_TILE,),
        in_specs=[pl.BlockSpec((ROW_TILE, d), lambda i: (i, 0)),
                  pl.BlockSpec((1, d), lambda i: (0, 0))],
        out_specs=pl.BlockSpec((ROW_TILE, d), lambda i: (i, 0)),
        out_shape=jax.ShapeDtypeStruct((rows, d), BF16),
        compiler_params=_params(("arbitrary",)),
        name="rmsnorm",
    )(x, w.reshape(1, d))


def _proj_kernel(a_ref, w_ref, o_ref, wb_ref, *, act):
    _cast_weight(w_ref, wb_ref)
    acc = _dot(a_ref[...], wb_ref[...])
    if act == "silu":
        acc = _silu(acc)
    o_ref[...] = acc.astype(o_ref.dtype)


def _proj(a, w, layer, col0, ncols, out_dtype, act=None, name="proj"):
    rows, k = a.shape
    off = col0 // COL_TILE
    return pl.pallas_call(
        functools.partial(_proj_kernel, act=act),
        grid=(ncols // COL_TILE, rows // ROW_TILE),
        in_specs=[pl.BlockSpec((ROW_TILE, k), lambda j, i: (i, 0)),
                  pl.BlockSpec((None, k, COL_TILE), lambda j, i: (layer, 0, j + off))],
        out_specs=pl.BlockSpec((ROW_TILE, COL_TILE), lambda j, i: (i, j)),
        out_shape=jax.ShapeDtypeStruct((rows, ncols), out_dtype),
        scratch_shapes=[pltpu.VMEM((k, COL_TILE), BF16)],
        compiler_params=_params(("arbitrary", "arbitrary")),
        name=name,
    )(a, w)


def _qk_kernel(a_ref, w_ref, cos_ref, sin_ref, o_ref, wb_ref, *, nq_tiles):
    _cast_weight(w_ref, wb_ref)
    acc = _dot(a_ref[...], wb_ref[...])
    scale = jnp.where(pl.program_id(0) >= nq_tiles, RET_DK ** -0.5, 1.0).astype(F32)
    cos = cos_ref[...] * scale
    sin = sin_ref[...] * scale
    half = RET_DK // 2
    for hd in range(COL_TILE // RET_DK):
        lo = hd * RET_DK
        x1 = acc[:, lo:lo + half]
        x2 = acc[:, lo + half:lo + RET_DK]
        o_ref[:, lo:lo + half] = x1 * cos - x2 * sin
        o_ref[:, lo + half:lo + RET_DK] = x1 * sin + x2 * cos


def _qk_proj(a, w, layer, cos, sin, tiles_per_batch):
    rows, k = a.shape
    half = RET_DK // 2
    return pl.pallas_call(
        functools.partial(_qk_kernel, nq_tiles=RET_QK // COL_TILE),
        grid=(2 * RET_QK // COL_TILE, rows // ROW_TILE),
        in_specs=[pl.BlockSpec((ROW_TILE, k), lambda j, i: (i, 0)),
                  pl.BlockSpec((None, k, COL_TILE), lambda j, i: (layer, 0, j)),
                  pl.BlockSpec((ROW_TILE, half), lambda j, i: (i % tiles_per_batch, 0)),
                  pl.BlockSpec((ROW_TILE, half), lambda j, i: (i % tiles_per_batch, 0))],
        out_specs=pl.BlockSpec((ROW_TILE, COL_TILE), lambda j, i: (i, j)),
        out_shape=jax.ShapeDtypeStruct((rows, 2 * RET_QK), F32),
        scratch_shapes=[pltpu.VMEM((k, COL_TILE), BF16)],
        compiler_params=_params(("arbitrary", "arbitrary")),
        name="ret_qk_proj",
    )(a, w, cos, sin)


def _causal_conv(buf_ref, cw_ref, cb_ref, rows, width):
    cw = cw_ref[...]
    out = cb_ref[...] + cw[width - 1:width, :] * buf_ref[CARRY:CARRY + rows, :]
    for tap in range(width - 1):
        shift = width - 1 - tap
        out = out + cw[tap:tap + 1, :] * buf_ref[CARRY - shift:CARRY - shift + rows, :]
    return out


def _conv_carry_reset(buf_ref, tiles_per_batch):
    @pl.when(pl.program_id(1) % tiles_per_batch == 0)
    def _():
        buf_ref[0:CARRY, :] = jnp.zeros((CARRY, buf_ref.shape[1]), F32)


def _conv_carry_save(buf_ref, rows):
    buf_ref[0:CARRY, :] = buf_ref[rows:rows + CARRY, :]


def _ffn_up_kernel(a_ref, wg_ref, wu_ref, cw_ref, cb_ref, o_ref, wgb_ref, wub_ref, buf_ref,
                   *, tiles_per_batch, width):
    _cast_weight(wg_ref, wgb_ref)
    _cast_weight(wu_ref, wub_ref)
    _conv_carry_reset(buf_ref, tiles_per_batch)
    rows = a_ref.shape[0]
    a = a_ref[...]
    buf_ref[CARRY:CARRY + rows, :] = _dot(a, wgb_ref[...])
    up = _dot(a, wub_ref[...])
    gate = _causal_conv(buf_ref, cw_ref, cb_ref, rows, width)
    o_ref[...] = (_silu(gate) * up).astype(o_ref.dtype)
    _conv_carry_save(buf_ref, rows)


def _ffn_up(a, w_up, layer, conv_w, conv_b, tiles_per_batch):
    rows, k = a.shape
    width = conv_w.shape[0]
    nt = FFN_DIM // COL_TILE
    return pl.pallas_call(
        functools.partial(_ffn_up_kernel, tiles_per_batch=tiles_per_batch, width=width),
        grid=(nt, rows // ROW_TILE),
        in_specs=[pl.BlockSpec((ROW_TILE, k), lambda j, i: (i, 0)),
                  pl.BlockSpec((None, k, COL_TILE), lambda j, i: (layer, 0, j)),
                  pl.BlockSpec((None, k, COL_TILE), lambda j, i: (layer, 0, j + nt)),
                  pl.BlockSpec((width, COL_TILE), lambda j, i: (0, j)),
                  pl.BlockSpec((1, COL_TILE), lambda j, i: (0, j))],
        out_specs=pl.BlockSpec((ROW_TILE, COL_TILE), lambda j, i: (i, j)),
        out_shape=jax.ShapeDtypeStruct((rows, FFN_DIM), BF16),
        scratch_shapes=[pltpu.VMEM((k, COL_TILE), BF16), pltpu.VMEM((k, COL_TILE), BF16),
                        pltpu.VMEM((CARRY + ROW_TILE, COL_TILE), F32)],
        compiler_params=_params(("arbitrary", "arbitrary")),
        name="ffn_up",
    )(a, w_up, w_up, conv_w, conv_b.reshape(1, FFN_DIM))


def _xbc_kernel(a_ref, w_ref, cw_ref, cb_ref, m_ref, o_ref, wb_ref, buf_ref,
                *, tiles_per_batch, width, nx_tiles):
    _cast_weight(w_ref, wb_ref)
    _conv_carry_reset(buf_ref, tiles_per_batch)
    rows = a_ref.shape[0]
    buf_ref[CARRY:CARRY + rows, :] = _dot(a_ref[...], wb_ref[...])
    y = _silu(_causal_conv(buf_ref, cw_ref, cb_ref, rows, width))
    mask = jnp.where(pl.program_id(0) < nx_tiles, m_ref[...], 1.0)
    o_ref[...] = y * mask
    _conv_carry_save(buf_ref, rows)


def _xbc_proj(a, w_in, layer, conv_w, conv_b, vmask, tiles_per_batch):
    rows, k = a.shape
    width = conv_w.shape[0]
    off = SSD_DI // COL_TILE
    return pl.pallas_call(
        functools.partial(_xbc_kernel, tiles_per_batch=tiles_per_batch, width=width,
                          nx_tiles=SSD_DI // COL_TILE),
        grid=(SSD_CONV_DIM // COL_TILE, rows // ROW_TILE),
        in_specs=[pl.BlockSpec((ROW_TILE, k), lambda j, i: (i, 0)),
                  pl.BlockSpec((None, k, COL_TILE), lambda j, i: (layer, 0, j + off)),
                  pl.BlockSpec((width, COL_TILE), lambda j, i: (0, j)),
                  pl.BlockSpec((1, COL_TILE), lambda j, i: (0, j)),
                  pl.BlockSpec((ROW_TILE, 1), lambda j, i: (i, 0))],
        out_specs=pl.BlockSpec((ROW_TILE, COL_TILE), lambda j, i: (i, j)),
        out_shape=jax.ShapeDtypeStruct((rows, SSD_CONV_DIM), F32),
        scratch_shapes=[pltpu.VMEM((k, COL_TILE), BF16),
                        pltpu.VMEM((CARRY + ROW_TILE, COL_TILE), F32)],
        compiler_params=_params(("arbitrary", "arbitrary")),
        name="ssd_xbc_proj",
    )(a, w_in, conv_w, conv_b.reshape(1, SSD_CONV_DIM), vmask)


def _dt_kernel(a_ref, w_ref, bias_ref, alog_ref, tri_ref, exp_ref, dte_ref, cume_ref):
    dt = _softplus(_dot(a_ref[...], w_ref[...].astype(BF16)) + bias_ref[...])
    da = dt * (-jnp.exp(alog_ref[...]))
    cum = _dot01(tri_ref[...], da, sel_first=True)
    sel = exp_ref[...]
    dte_ref[...] = _dot01(sel, dt, sel_first=False)
    cume_ref[...] = _dot01(sel, cum, sel_first=False)


def _dt_path(a, w_dt, dt_bias, a_log):
    rows, k = a.shape
    r = np.arange(ROW_TILE)
    tri = ((r[:, None] // CHUNK == r[None, :] // CHUNK) & (r[:, None] >= r[None, :]))
    tri = jnp.asarray(tri, BF16)
    expand = jnp.asarray(np.arange(SSD_HEADS)[:, None] == (np.arange(SSD_DI)[None, :] // SSD_HEADDIM), BF16)
    shp = jax.ShapeDtypeStruct((rows, SSD_DI), F32)
    return pl.pallas_call(
        _dt_kernel,
        grid=(rows // ROW_TILE, SSD_GROUPS),
        in_specs=[pl.BlockSpec((ROW_TILE, k), lambda i, g: (i, 0)),
                  pl.BlockSpec((k, SSD_HEADS), lambda i, g: (0, 0)),
                  pl.BlockSpec((1, SSD_HEADS), lambda i, g: (0, 0)),
                  pl.BlockSpec((1, SSD_HEADS), lambda i, g: (0, 0)),
                  pl.BlockSpec((ROW_TILE, ROW_TILE), lambda i, g: (0, 0)),
                  pl.BlockSpec((SSD_HEADS, SSD_GW), lambda i, g: (0, g))],
        out_specs=[pl.BlockSpec((ROW_TILE, SSD_GW), lambda i, g: (i, g)),
                   pl.BlockSpec((ROW_TILE, SSD_GW), lambda i, g: (i, g))],
        out_shape=[shp, shp],
        compiler_params=_params(("arbitrary", "arbitrary")),
        name="ssd_dt_path",
    )(a, w_dt, dt_bias.reshape(1, SSD_HEADS), a_log.reshape(1, SSD_HEADS), tri, expand)


def _out_kernel(a_ref, w_ref, h_ref, m_ref, nw_ref, *refs, nk, keep_residual):
    if keep_residual:
        acc_ref, hn_ref = refs
    else:
        hn_ref, acc_ref = refs
    kk = pl.program_id(1)
    part = _dot(a_ref[...], w_ref[...].astype(BF16))

    @pl.when(kk == 0)
    def _():
        acc_ref[...] = part

    @pl.when(kk > 0)
    def _():
        acc_ref[...] += part

    @pl.when(kk == nk - 1)
    def _():
        hnew = (h_ref[...] + acc_ref[...]) * m_ref[...]
        if keep_residual:
            acc_ref[...] = hnew
        hn_ref[...] = _rms_scale(hnew, nw_ref[...]).astype(hn_ref.dtype)


def _out_proj(a, w, layer, h, vmask, norm_w, hn_dtype, name, keep_residual=True):
    rows, k = a.shape
    d = w.shape[2]
    nk = k // K_TILE
    row_blk = pl.BlockSpec((ROW_TILE, d), lambda i, kk: (i, 0))
    hn_shape = jax.ShapeDtypeStruct((rows, d), hn_dtype)
    if keep_residual:
        out_specs, out_shape, scratch = [row_blk, row_blk], [jax.ShapeDtypeStruct((rows, d), F32), hn_shape], []
    else:
        out_specs, out_shape, scratch = [row_blk], [hn_shape], [pltpu.VMEM((ROW_TILE, d), F32)]
    return pl.pallas_call(
        functools.partial(_out_kernel, nk=nk, keep_residual=keep_residual),
        grid=(rows // ROW_TILE, nk),
        in_specs=[pl.BlockSpec((ROW_TILE, K_TILE), lambda i, kk: (i, kk)),
                  pl.BlockSpec((None, K_TILE, d), lambda i, kk: (layer, kk, 0)),
                  row_blk,
                  pl.BlockSpec((ROW_TILE, 1), lambda i, kk: (i, 0)),
                  pl.BlockSpec((1, d), lambda i, kk: (0, 0))],
        out_specs=out_specs,
        out_shape=out_shape,
        scratch_shapes=scratch,
        compiler_params=_params(("arbitrary", "arbitrary")),
        name=name,
    )(a, w, h, vmask, norm_w.reshape(1, d))


def _ret_scan_kernel(q_ref, k_ref, v_ref, g_ref, gnw_ref, intra_ref, qd_ref, kd_ref, cd_ref,
                     y_ref, state_ref, *, chunks):
    @pl.when(pl.program_id(2) == 0)
    def _():
        state_ref[...] = jnp.zeros(state_ref.shape, F32)

    intra = intra_ref[0]
    q_decay = qd_ref[0]
    k_decay = kd_ref[0]
    chunk_decay = cd_ref[0]
    gnw = gnw_ref[...]

    def body(c, carry):
        r0 = pl.multiple_of(c * CHUNK, CHUNK)
        q = q_ref[pl.ds(r0, CHUNK), :]
        k = k_ref[pl.ds(r0, CHUNK), :]
        v = v_ref[pl.ds(r0, CHUNK), :]
        qb = q.astype(BF16)
        scores = _dot_nt(qb, k.astype(BF16)) * intra
        o = _dot(scores.astype(BF16), v)
        state = state_ref[...]
        o = o + q_decay * _dot(qb, state.astype(BF16))
        kd = (k * k_decay).astype(BF16)
        state_ref[...] = state * chunk_decay + _dot_tn(kd, v)
        o = o * lax.rsqrt(jnp.mean(o * o, axis=-1, keepdims=True) + NORM_EPS) * gnw
        y_ref[pl.ds(r0, CHUNK), :] = (_silu(g_ref[pl.ds(r0, CHUNK), :]) * o).astype(y_ref.dtype)
        return carry

    lax.fori_loop(0, chunks, body, 0)


def _ret_scan(qk, v, g, gn_w, bsz, tiles_per_batch):
    rows = qk.shape[0]
    log_gamma = np.log1p(-np.exp2(-5.0 - np.arange(RET_HEADS, dtype=np.float64)))
    idx = np.arange(CHUNK, dtype=np.float64)
    diff = idx[:, None] - idx[None, :]
    intra = np.where(diff[None] >= 0, np.exp(np.maximum(diff, 0.0)[None] * log_gamma[:, None, None]), 0.0)
    q_decay = np.exp((idx + 1.0)[None, :] * log_gamma[:, None])[..., None]
    k_decay = np.exp((CHUNK - 1.0 - idx)[None, :] * log_gamma[:, None])[..., None]
    chunk_decay = np.exp(CHUNK * log_gamma)[:, None, None]
    intra, q_decay, k_decay, chunk_decay = (jnp.asarray(t, F32) for t in (intra, q_decay, k_decay, chunk_decay))
    tpb = tiles_per_batch
    nkh = RET_QK // RET_DK
    return pl.pallas_call(
        functools.partial(_ret_scan_kernel, chunks=ROW_TILE // CHUNK),
        grid=(bsz, RET_HEADS, tpb),
        in_specs=[pl.BlockSpec((ROW_TILE, RET_DK), lambda b, h, t: (b * tpb + t, h)),
                  pl.BlockSpec((ROW_TILE, RET_DK), lambda b, h, t: (b * tpb + t, nkh + h)),
                  pl.BlockSpec((ROW_TILE, RET_DV), lambda b, h, t: (b * tpb + t, h)),
                  pl.BlockSpec((ROW_TILE, RET_DV), lambda b, h, t: (b * tpb + t, h)),
                  pl.BlockSpec((1, RET_DV), lambda b, h, t: (0, h)),
                  pl.BlockSpec((1, CHUNK, CHUNK), lambda b, h, t: (h, 0, 0)),
                  pl.BlockSpec((1, CHUNK, 1), lambda b, h, t: (h, 0, 0)),
                  pl.BlockSpec((1, CHUNK, 1), lambda b, h, t: (h, 0, 0)),
                  pl.BlockSpec((1, 1, 1), lambda b, h, t: (h, 0, 0))],
        out_specs=pl.BlockSpec((ROW_TILE, RET_DV), lambda b, h, t: (b * tpb + t, h)),
        out_shape=jax.ShapeDtypeStruct((rows, RET_VDIM), BF16),
        scratch_shapes=[pltpu.VMEM((RET_DK, RET_DV), F32)],
        compiler_params=_params(("arbitrary", "arbitrary", "arbitrary")),
        name="ret_scan",
    )(qk, qk, v, g, gn_w.reshape(1, RET_VDIM), intra, q_decay, k_decay, chunk_decay)


def _ssd_scan_kernel(x_ref, b_ref, c_ref, z_ref, dte_ref, cume_ref, d_ref, gw_ref,
                     y_ref, state_ref, *, chunks):
    @pl.when(pl.program_id(2) == 0)
    def _():
        state_ref[...] = jnp.zeros(state_ref.shape, F32)

    gwid = SSD_GW
    row = lax.broadcasted_iota(jnp.int32, (CHUNK, gwid), 0)
    lane_pos = lax.broadcasted_iota(jnp.int32, (CHUNK, gwid), 1) & (CHUNK - 1)
    causal = row >= lane_pos
    diag = row == lane_pos
    quad = 4 * SSD_HEADDIM
    blk_r = lax.broadcasted_iota(jnp.int32, (quad, quad), 0) // SSD_HEADDIM
    blk_c = lax.broadcasted_iota(jnp.int32, (quad, quad), 1) // SSD_HEADDIM
    blockdiag = blk_r == blk_c
    d_skip = d_ref[...]
    gw = gw_ref[...]

    def body(c, carry):
        r0 = pl.multiple_of(c * CHUNK, CHUNK)
        rows = pl.ds(r0, CHUNK)
        x = x_ref[rows, :]
        cum = cume_ref[rows, :]
        bb = b_ref[rows, :].astype(BF16)
        cb = c_ref[rows, :].astype(BF16)
        xdt = x * dte_ref[rows, :]
        cum_row = jnp.sum(jnp.where(diag, cum, 0.0), axis=0, keepdims=True)
        seg = jnp.where(causal, cum - cum_row, 0.0)
        decay = jnp.where(causal, jnp.exp(seg), 0.0)
        gram = _dot_nt(cb, jnp.concatenate([bb] * SSD_HPG, axis=0))
        attn = (gram * decay).astype(BF16)
        xdt_b = xdt.astype(BF16)
        ys = []
        for qd in range(gwid // quad):
            xq = xdt_b[:, qd * quad:(qd + 1) * quad]
            rhs = jnp.where(blockdiag, jnp.concatenate([xq] * 4, axis=0), jnp.zeros((), BF16))
            ys.append(_dot(attn[:, qd * quad:(qd + 1) * quad], rhs))
        y = jnp.concatenate(ys, axis=1)
        state = state_ref[...]
        y = y + _dot(cb, state.astype(BF16)) * jnp.exp(cum)
        cum_last = cum[CHUNK - 1:CHUNK, :]
        xw = (xdt * jnp.exp(cum_last - cum)).astype(BF16)
        state_ref[...] = state * jnp.exp(cum_last) + _dot_tn(bb, xw)
        y = (y + x * d_skip) * z_ref[rows, :]
        y = y * lax.rsqrt(jnp.mean(y * y, axis=-1, keepdims=True) + NORM_EPS) * gw
        y_ref[rows, :] = y.astype(y_ref.dtype)
        return carry

    lax.fori_loop(0, chunks, body, 0)


def _ssd_scan(xbc, zs, dte, cume, d_skip, gnorm_w, bsz, tiles_per_batch):
    rows = xbc.shape[0]
    tpb = tiles_per_batch
    b_blk = SSD_DI // SSD_STATE
    c_blk = (SSD_DI + SSD_BC) // SSD_STATE
    d_exp = jnp.repeat(d_skip, SSD_HEADDIM).reshape(1, SSD_DI)
    wide = pl.BlockSpec((ROW_TILE, SSD_GW), lambda b, g, t: (b * tpb + t, g))
    return pl.pallas_call(
        functools.partial(_ssd_scan_kernel, chunks=ROW_TILE // CHUNK),
        grid=(bsz, SSD_GROUPS, tpb),
        in_specs=[wide,
                  pl.BlockSpec((ROW_TILE, SSD_STATE), lambda b, g, t: (b * tpb + t, b_blk + g)),
                  pl.BlockSpec((ROW_TILE, SSD_STATE), lambda b, g, t: (b * tpb + t, c_blk + g)),
                  wide, wide, wide,
                  pl.BlockSpec((1, SSD_GW), lambda b, g, t: (0, g)),
                  pl.BlockSpec((1, SSD_GW), lambda b, g, t: (0, g))],
        out_specs=wide,
        out_shape=jax.ShapeDtypeStruct((rows, SSD_DI), BF16),
        scratch_shapes=[pltpu.VMEM((SSD_STATE, SSD_GW), F32)],
        compiler_params=_params(("arbitrary", "arbitrary", "arbitrary")),
        name="ssd_scan",
    )(xbc, xbc, xbc, zs, dte, cume, d_exp, gnorm_w.reshape(1, SSD_DI))


def kernel(x, meta_tokens, ret_norm_w, ret_w_in, ret_gn_w, ret_w_out, ssd_norm_w, ssd_w_in, ssd_conv_w, ssd_conv_b, ssd_dt_bias, ssd_a_log, ssd_d, ssd_gnorm_w, ssd_w_out, ffn_norm_w, ffn_w_up, ffn_conv_w, ffn_conv_b, ffn_w_down, final_norm_w):
    bsz, seq, d = x.shape
    length = META_PAD + N_META + seq
    assert d == D_MODEL and length % ROW_TILE == 0 and ROW_TILE % CHUNK == 0
    tpb = length // ROW_TILE
    depth = ffn_w_up.shape[0]

    meta = jnp.broadcast_to(meta_tokens.astype(x.dtype)[None], (bsz, N_META, d))
    h = jnp.concatenate([jnp.zeros((bsz, META_PAD, d), x.dtype), meta, x], axis=1).reshape(bsz * length, d)

    pos_i = np.arange(length) - META_PAD
    vmask = jnp.asarray(np.tile((pos_i >= 0).astype(np.float32), bsz).reshape(bsz * length, 1))
    half = RET_DK // 2
    inv = ROPE_BASE ** (-jnp.arange(half, dtype=F32) / half)
    ang = jnp.asarray(pos_i, F32)[:, None] * inv[None, :]
    cos, sin = jnp.cos(ang), jnp.sin(ang)

    hn = _rmsnorm(h, ret_norm_w[0])
    for i in range(depth):
        j = i // 2
        if i % 2 == 0:
            qk = _qk_proj(hn, ret_w_in, j, cos, sin, tpb)
            v = _proj(hn, ret_w_in, j, 2 * RET_QK, RET_VDIM, BF16, name="ret_v_proj")
            g = _proj(hn, ret_w_in, j, 2 * RET_QK + RET_VDIM, RET_VDIM, F32, name="ret_g_proj")
            y = _ret_scan(qk, v, g, ret_gn_w[j], bsz, tpb)
            h, hn = _out_proj(y, ret_w_out, j, h, vmask, ffn_norm_w[i], BF16, name="ret_out_proj")
        else:
            zs = _proj(hn, ssd_w_in, j, 0, SSD_DI, F32, act="silu", name="ssd_z_proj")
            xbc = _xbc_proj(hn, ssd_w_in, j, ssd_conv_w[j], ssd_conv_b[j], vmask, tpb)
            w_dt = ssd_w_in[j, :, SSD_DI + SSD_CONV_DIM:]
            dte, cume = _dt_path(hn, w_dt, ssd_dt_bias[j], ssd_a_log[j])
            y = _ssd_scan(xbc, zs, dte, cume, ssd_d[j], ssd_gnorm_w[j], bsz, tpb)
            h, hn = _out_proj(y, ssd_w_out, j, h, vmask, ffn_norm_w[i], BF16, name="ssd_out_proj")
        act = _ffn_up(hn, ffn_w_up, i, ffn_conv_w[i], ffn_conv_b[i], tpb)
        if i == depth - 1:
            (hn,) = _out_proj(act, ffn_w_down, i, h, vmask, final_norm_w, x.dtype,
                              name="ffn_down_final", keep_residual=False)
        else:
            next_w = ssd_norm_w[(i + 1) // 2] if (i + 1) % 2 == 1 else ret_norm_w[(i + 1) // 2]
            h, hn = _out_proj(act, ffn_w_down, i, h, vmask, next_w, BF16, name="ffn_down_proj")
    return hn.reshape(bsz, length, d)[:, META_PAD + N_META:]
```

```python
import functools

import jax
import jax.numpy as jnp
import numpy as np
from jax import lax
from jax.experimental import pallas as pl
from jax.experimental.pallas import tpu as pltpu

F32 = jnp.float32
BF16 = jnp.bfloat16

D_MODEL = 2048
CHUNK = 64
N_META = 16
META_PAD = CHUNK - N_META
NORM_EPS = 1e-6

RET_HEADS = 8
RET_DK = 256
RET_DV = 512
RET_QK = RET_HEADS * RET_DK
RET_VDIM = RET_HEADS * RET_DV
ROPE_BASE = 10000.0

SSD_DI = 4096
SSD_HEADDIM = 64
SSD_HEADS = 64
SSD_GROUPS = 8
SSD_HPG = 8
SSD_STATE = 128
SSD_BC = SSD_GROUPS * SSD_STATE
SSD_CONV_DIM = SSD_DI + 2 * SSD_BC
SSD_GW = SSD_HPG * SSD_HEADDIM

FFN_DIM = 5632

V7X_VMEM_BYTES = 64 * 1024 * 1024
VMEM_LIMIT = V7X_VMEM_BYTES - 8 * 1024 * 1024

ROW_TILE = 832
COL_TILE = 512
W_CHUNK = 256
CARRY = 8


def _params(sem):
    return pltpu.CompilerParams(dimension_semantics=sem, vmem_limit_bytes=VMEM_LIMIT)


def _sigmoid(x):
    return 1.0 / (1.0 + jnp.exp(-x))


def _silu(x):
    return x * _sigmoid(x)


def _softplus(x):
    return jnp.maximum(x, 0.0) + jnp.log1p(jnp.exp(-jnp.abs(x)))


def _dot(a, b):
    return jnp.dot(a, b, preferred_element_type=F32)


def _dot_nt(a, b):
    return lax.dot_general(a, b, (((1,), (1,)), ((), ())), preferred_element_type=F32)


def _dot_tn(a, b):
    return lax.dot_general(a, b, (((0,), (0,)), ((), ())), preferred_element_type=F32)


def _split3(x):
    hi = x.astype(BF16)
    r1 = x - hi.astype(F32)
    mid = r1.astype(BF16)
    lo = (r1 - mid.astype(F32)).astype(BF16)
    return hi, mid, lo


def _dot01(sel, x, sel_first):
    parts = _split3(x)
    if sel_first:
        return _dot(sel, parts[0]) + _dot(sel, parts[1]) + _dot(sel, parts[2])
    return _dot(parts[0], sel) + _dot(parts[1], sel) + _dot(parts[2], sel)


def _rms_scale(x, w):
    ms = jnp.mean(x * x, axis=-1, keepdims=True)
    return x * lax.rsqrt(ms + NORM_EPS) * w


def _cast_weight(w_ref, wb_ref):
    @pl.when(pl.program_id(1) == 0)
    def _():
        wb_ref[...] = w_ref[...].astype(BF16)


def _rmsnorm_kernel(x_ref, w_ref, o_ref):
    o_ref[...] = _rms_scale(x_ref[...], w_ref[...]).astype(o_ref.dtype)


def _rmsnorm(x, w):
    rows, d = x.shape
    return pl.pallas_call(
        _rmsnorm_kernel,
        grid=(rows // ROW_TILE,),
        in_specs=[pl.BlockSpec((ROW_TILE, d), lambda i: (i, 0)),
                  pl.BlockSpec((1, d), lambda i: (0, 0))],
        out_specs=pl.BlockSpec((ROW_TILE, d), lambda i: (i, 0)),
        out_shape=jax.ShapeDtypeStruct((rows, d), BF16),
        compiler_params=_params(("arbitrary",)),
        name="rmsnorm",
    )(x, w.reshape(1, d))


def _proj_kernel(a_ref, w_ref, o_ref, wb_ref, *, act):
    _cast_weight(w_ref, wb_ref)
    acc = _dot(a_ref[...], wb_ref[...])
    if act == "silu":
        acc = _silu(acc)
    o_ref[...] = acc.astype(o_ref.dtype)


def _proj(a, w, layer, col0, ncols, out_dtype, act=None, name="proj"):
    rows, k = a.shape
    off = col0 // COL_TILE
    return pl.pallas_call(
        functools.partial(_proj_kernel, act=act),
        grid=(ncols // COL_TILE, rows // ROW_TILE),
        in_specs=[pl.BlockSpec((ROW_TILE, k), lambda j, i: (i, 0)),
                  pl.BlockSpec((None, k, COL_TILE), lambda j, i: (layer, 0, j + off))],
        out_specs=pl.BlockSpec((ROW_TILE, COL_TILE), lambda j, i: (i, j)),
        out_shape=jax.ShapeDtypeStruct((rows, ncols), out_dtype),
        scratch_shapes=[pltpu.VMEM((k, COL_TILE), BF16)],
        compiler_params=_params(("arbitrary", "arbitrary")),
        name=name,
    )(a, w)


def _qk_kernel(a_ref, w_ref, cos_ref, sin_ref, o_ref, wb_ref, *, nq_tiles):
    _cast_weight(w_ref, wb_ref)
    acc = _dot(a_ref[...], wb_ref[...])
    scale = jnp.where(pl.program_id(0) >= nq_tiles, RET_DK ** -0.5, 1.0).astype(F32)
    cos = cos_ref[...] * scale
    sin = sin_ref[...] * scale
    half = RET_DK // 2
    for hd in range(COL_TILE // RET_DK):
        lo = hd * RET_DK
        x1 = acc[:, lo:lo + half]
        x2 = acc[:, lo + half:lo + RET_DK]
        o_ref[:, lo:lo + half] = x1 * cos - x2 * sin
        o_ref[:, lo + half:lo + RET_DK] = x1 * sin + x2 * cos


def _qk_proj(a, w, layer, cos, sin, tiles_per_batch):
    rows, k = a.shape
    half = RET_DK // 2
    return pl.pallas_call(
        functools.partial(_qk_kernel, nq_tiles=RET_QK // COL_TILE),
        grid=(2 * RET_QK // COL_TILE, rows // ROW_TILE),
        in_specs=[pl.BlockSpec((ROW_TILE, k), lambda j, i: (i, 0)),
                  pl.BlockSpec((None, k, COL_TILE), lambda j, i: (layer, 0, j)),
                  pl.BlockSpec((ROW_TILE, half), lambda j, i: (i % tiles_per_batch, 0)),
                  pl.BlockSpec((ROW_TILE, half), lambda j, i: (i % tiles_per_batch, 0))],
        out_specs=pl.BlockSpec((ROW_TILE, COL_TILE), lambda j, i: (i, j)),
        out_shape=jax.ShapeDtypeStruct((rows, 2 * RET_QK), F32),
        scratch_shapes=[pltpu.VMEM((k, COL_TILE), BF16)],
        compiler_params=_params(("arbitrary", "arbitrary")),
        name="ret_qk_proj",
    )(a, w, cos, sin)


def _causal_conv(buf_ref, cw_ref, cb_ref, rows, width):
    cw = cw_ref[...]
    out = cb_ref[...] + cw[width - 1:width, :] * buf_ref[CARRY:CARRY + rows, :]
    for tap in range(width - 1):
        shift = width - 1 - tap
        out = out + cw[tap:tap + 1, :] * buf_ref[CARRY - shift:CARRY - shift + rows, :]
    return out


def _conv_carry_reset(buf_ref, tiles_per_batch):
    @pl.when(pl.program_id(1) % tiles_per_batch == 0)
    def _():
        buf_ref[0:CARRY, :] = jnp.zeros((CARRY, buf_ref.shape[1]), F32)


def _conv_carry_save(buf_ref, rows):
    buf_ref[0:CARRY, :] = buf_ref[rows:rows + CARRY, :]


def _ffn_up_kernel(a_ref, wg_ref, wu_ref, cw_ref, cb_ref, o_ref, wgb_ref, wub_ref, buf_ref,
                   *, tiles_per_batch, width):
    _cast_weight(wg_ref, wgb_ref)
    _cast_weight(wu_ref, wub_ref)
    _conv_carry_reset(buf_ref, tiles_per_batch)
    rows = a_ref.shape[0]
    a = a_ref[...]
    buf_ref[CARRY:CARRY + rows, :] = _dot(a, wgb_ref[...])
    up = _dot(a, wub_ref[...])
    gate = _causal_conv(buf_ref, cw_ref, cb_ref, rows, width)
    o_ref[...] = (_silu(gate) * up).astype(o_ref.dtype)
    _conv_carry_save(buf_ref, rows)


def _ffn_up(a, w_up, layer, conv_w, conv_b, tiles_per_batch):
    rows, k = a.shape
    width = conv_w.shape[0]
    nt = FFN_DIM // COL_TILE
    return pl.pallas_call(
        functools.partial(_ffn_up_kernel, tiles_per_batch=tiles_per_batch, width=width),
        grid=(nt, rows // ROW_TILE),
        in_specs=[pl.BlockSpec((ROW_TILE, k), lambda j, i: (i, 0)),
                  pl.BlockSpec((None, k, COL_TILE), lambda j, i: (layer, 0, j)),
                  pl.BlockSpec((None, k, COL_TILE), lambda j, i: (layer, 0, j + nt)),
                  pl.BlockSpec((width, COL_TILE), lambda j, i: (0, j)),
                  pl.BlockSpec((1, COL_TILE), lambda j, i: (0, j))],
        out_specs=pl.BlockSpec((ROW_TILE, COL_TILE), lambda j, i: (i, j)),
        out_shape=jax.ShapeDtypeStruct((rows, FFN_DIM), BF16),
        scratch_shapes=[pltpu.VMEM((k, COL_TILE), BF16), pltpu.VMEM((k, COL_TILE), BF16),
                        pltpu.VMEM((CARRY + ROW_TILE, COL_TILE), F32)],
        compiler_params=_params(("arbitrary", "arbitrary")),
        name="ffn_up",
    )(a, w_up, w_up, conv_w, conv_b.reshape(1, FFN_DIM))


def _xbc_kernel(a_ref, w_ref, cw_ref, cb_ref, m_ref, o_ref, wb_ref, buf_ref,
                *, tiles_per_batch, width, nx_tiles):
    _cast_weight(w_ref, wb_ref)
    _conv_carry_reset(buf_ref, tiles_per_batch)
    rows = a_ref.shape[0]
    buf_ref[CARRY:CARRY + rows, :] = _dot(a_ref[...], wb_ref[...])
    y = _silu(_causal_conv(buf_ref, cw_ref, cb_ref, rows, width))
    mask = jnp.where(pl.program_id(0) < nx_tiles, m_ref[...], 1.0)
    o_ref[...] = y * mask
    _conv_carry_save(buf_ref, rows)


def _xbc_proj(a, w_in, layer, conv_w, conv_b, vmask, tiles_per_batch):
    rows, k = a.shape
    width = conv_w.shape[0]
    off = SSD_DI // COL_TILE
    return pl.pallas_call(
        functools.partial(_xbc_kernel, tiles_per_batch=tiles_per_batch, width=width,
                          nx_tiles=SSD_DI // COL_TILE),
        grid=(SSD_CONV_DIM // COL_TILE, rows // ROW_TILE),
        in_specs=[pl.BlockSpec((ROW_TILE, k), lambda j, i: (i, 0)),
                  pl.BlockSpec((None, k, COL_TILE), lambda j, i: (layer, 0, j + off)),
                  pl.BlockSpec((width, COL_TILE), lambda j, i: (0, j)),
                  pl.BlockSpec((1, COL_TILE), lambda j, i: (0, j)),
                  pl.BlockSpec((ROW_TILE, 1), lambda j, i: (i, 0))],
        out_specs=pl.BlockSpec((ROW_TILE, COL_TILE), lambda j, i: (i, j)),
        out_shape=jax.ShapeDtypeStruct((rows, SSD_CONV_DIM), F32),
        scratch_shapes=[pltpu.VMEM((k, COL_TILE), BF16),
                        pltpu.VMEM((CARRY + ROW_TILE, COL_TILE), F32)],
        compiler_params=_params(("arbitrary", "arbitrary")),
        name="ssd_xbc_proj",
    )(a, w_in, conv_w, conv_b.reshape(1, SSD_CONV_DIM), vmask)


def _dt_kernel(a_ref, w_ref, bias_ref, alog_ref, tri_ref, dt_ref, cum_ref):
    w = w_ref[:, 0:SSD_HEADS].astype(BF16)
    dt = _softplus(_dot(a_ref[...], w) + bias_ref[...])
    da = dt * (-jnp.exp(alog_ref[...]))
    dt_ref[...] = dt
    cum_ref[...] = _dot01(tri_ref[...], da, sel_first=True)


def _dt_path(a, w_in, layer, dt_bias, a_log):
    rows, k = a.shape
    lane = 128
    dt_col_blk = (SSD_DI + SSD_CONV_DIM) // lane
    assert dt_col_blk * lane == SSD_DI + SSD_CONV_DIM
    r = np.arange(ROW_TILE)
    tri = ((r[:, None] // CHUNK == r[None, :] // CHUNK) & (r[:, None] >= r[None, :]))
    tri = jnp.asarray(tri, BF16)
    shp = jax.ShapeDtypeStruct((rows, SSD_HEADS), F32)
    return pl.pallas_call(
        _dt_kernel,
        grid=(rows // ROW_TILE,),
        in_specs=[pl.BlockSpec((ROW_TILE, k), lambda i: (i, 0)),
                  pl.BlockSpec((None, k, lane), lambda i: (layer, 0, dt_col_blk)),
                  pl.BlockSpec((1, SSD_HEADS), lambda i: (0, 0)),
                  pl.BlockSpec((1, SSD_HEADS), lambda i: (0, 0)),
                  pl.BlockSpec((ROW_TILE, ROW_TILE), lambda i: (0, 0))],
        out_specs=[pl.BlockSpec((ROW_TILE, SSD_HEADS), lambda i: (i, 0)),
                   pl.BlockSpec((ROW_TILE, SSD_HEADS), lambda i: (i, 0))],
        out_shape=[shp, shp],
        compiler_params=_params(("arbitrary",)),
        name="ssd_dt_path",
    )(a, w_in, dt_bias.reshape(1, SSD_HEADS), a_log.reshape(1, SSD_HEADS), tri)


def _out_kernel(a_ref, w_ref, h_ref, m_ref, nw_ref, *refs, n_load, keep_residual):
    if keep_residual:
        ho_ref, hn_ref, wb_ref = refs
    else:
        hn_ref, wb_ref = refs
    step = pl.program_id(0)

    @pl.when(step < n_load)
    def _():
        r0 = pl.multiple_of(step * W_CHUNK, W_CHUNK)
        wb_ref[pl.ds(r0, W_CHUNK), :] = w_ref[...].astype(BF16)

    @pl.when(step >= n_load)
    def _():
        hnew = (h_ref[...] + _dot(a_ref[...], wb_ref[...])) * m_ref[...]
        if keep_residual:
            ho_ref[...] = hnew
        hn_ref[...] = _rms_scale(hnew, nw_ref[...]).astype(hn_ref.dtype)


def _out_row_tile(k):
    return 416 if k <= 4096 else 320


def _out_proj(a, w, layer, h, vmask, norm_w, hn_dtype, name, keep_residual=True):
    rows, k = a.shape
    d = w.shape[2]
    tm = _out_row_tile(k)
    n_load = k // W_CHUNK
    assert n_load * W_CHUNK == k and rows % tm == 0

    def row(step):
        return jnp.maximum(step - n_load, 0)

    row_blk = pl.BlockSpec((tm, d), lambda s: (row(s), 0))
    hn_shape = jax.ShapeDtypeStruct((rows, d), hn_dtype)
    if keep_residual:
        out_specs, out_shape = [row_blk, row_blk], [jax.ShapeDtypeStruct((rows, d), F32), hn_shape]
    else:
        out_specs, out_shape = [row_blk], [hn_shape]
    return pl.pallas_call(
        functools.partial(_out_kernel, n_load=n_load, keep_residual=keep_residual),
        grid=(n_load + rows // tm,),
        in_specs=[pl.BlockSpec((tm, k), lambda s: (row(s), 0)),
                  pl.BlockSpec((None, W_CHUNK, d), lambda s: (layer, jnp.minimum(s, n_load - 1), 0)),
                  row_blk,
                  pl.BlockSpec((tm, 1), lambda s: (row(s), 0)),
                  pl.BlockSpec((1, d), lambda s: (0, 0))],
        out_specs=out_specs,
        out_shape=out_shape,
        scratch_shapes=[pltpu.VMEM((k, d), BF16)],
        compiler_params=_params(("arbitrary",)),
        name=name,
    )(a, w, h, vmask, norm_w.reshape(1, d))


def _ret_scan_kernel(q_ref, k_ref, v_ref, g_ref, gnw_ref, intra_ref, qd_ref, kd_ref, cd_ref,
                     y_ref, state_ref, *, chunks):
    @pl.when(pl.program_id(2) == 0)
    def _():
        state_ref[...] = jnp.zeros(state_ref.shape, F32)

    intra = intra_ref[0]
    q_decay = qd_ref[0]
    k_decay = kd_ref[0]
    chunk_decay = cd_ref[0]
    gnw = gnw_ref[...]

    def body(c, carry):
        r0 = c * CHUNK
        q = q_ref[pl.ds(r0, CHUNK), :]
        k = k_ref[pl.ds(r0, CHUNK), :]
        v = v_ref[pl.ds(r0, CHUNK), :]
        qb = q.astype(BF16)
        scores = _dot_nt(qb, k.astype(BF16)) * intra
        o = _dot(scores.astype(BF16), v)
        state = state_ref[...]
        o = o + q_decay * _dot(qb, state.astype(BF16))
        kd = (k * k_decay).astype(BF16)
        state_ref[...] = state * chunk_decay + _dot_tn(kd, v)
        o = o * lax.rsqrt(jnp.mean(o * o, axis=-1, keepdims=True) + NORM_EPS) * gnw
        y_ref[pl.ds(r0, CHUNK), :] = (_silu(g_ref[pl.ds(r0, CHUNK), :]) * o).astype(y_ref.dtype)
        return carry

    for c in range(chunks):
        body(c, 0)


def _ret_scan(qk, v, g, gn_w, bsz, tiles_per_batch):
    rows = qk.shape[0]
    log_gamma = np.log1p(-np.exp2(-5.0 - np.arange(RET_HEADS, dtype=np.float64)))
    idx = np.arange(CHUNK, dtype=np.float64)
    diff = idx[:, None] - idx[None, :]
    intra = np.where(diff[None] >= 0, np.exp(np.maximum(diff, 0.0)[None] * log_gamma[:, None, None]), 0.0)
    q_decay = np.exp((idx + 1.0)[None, :] * log_gamma[:, None])[..., None]
    k_decay = np.exp((CHUNK - 1.0 - idx)[None, :] * log_gamma[:, None])[..., None]
    chunk_decay = np.exp(CHUNK * log_gamma)[:, None, None]
    intra, q_decay, k_decay, chunk_decay = (jnp.asarray(t, F32) for t in (intra, q_decay, k_decay, chunk_decay))
    tpb = tiles_per_batch
    nkh = RET_QK // RET_DK
    return pl.pallas_call(
        functools.partial(_ret_scan_kernel, chunks=ROW_TILE // CHUNK),
        grid=(bsz, RET_HEADS, tpb),
        in_specs=[pl.BlockSpec((ROW_TILE, RET_DK), lambda b, h, t: (b * tpb + t, h)),
                  pl.BlockSpec((ROW_TILE, RET_DK), lambda b, h, t: (b * tpb + t, nkh + h)),
                  pl.BlockSpec((ROW_TILE, RET_DV), lambda b, h, t: (b * tpb + t, h)),
                  pl.BlockSpec((ROW_TILE, RET_DV), lambda b, h, t: (b * tpb + t, h)),
                  pl.BlockSpec((1, RET_DV), lambda b, h, t: (0, h)),
                  pl.BlockSpec((1, CHUNK, CHUNK), lambda b, h, t: (h, 0, 0)),
                  pl.BlockSpec((1, CHUNK, 1), lambda b, h, t: (h, 0, 0)),
                  pl.BlockSpec((1, CHUNK, 1), lambda b, h, t: (h, 0, 0)),
                  pl.BlockSpec((1, 1, 1), lambda b, h, t: (h, 0, 0))],
        out_specs=pl.BlockSpec((ROW_TILE, RET_DV), lambda b, h, t: (b * tpb + t, h)),
        out_shape=jax.ShapeDtypeStruct((rows, RET_VDIM), BF16),
        scratch_shapes=[pltpu.VMEM((RET_DK, RET_DV), F32)],
        compiler_params=_params(("arbitrary", "arbitrary", "arbitrary")),
        name="ret_scan",
    )(qk, qk, v, g, gn_w.reshape(1, RET_VDIM), intra, q_decay, k_decay, chunk_decay)


def _ssd_scan_kernel(x_ref, b_ref, c_ref, z_ref, dt_ref, cum_ref, sel_ref, d_ref, gw_ref,
                     y_ref, state_ref, dte_ref, cume_ref, *, chunks):
    @pl.when(pl.program_id(2) == 0)
    def _():
        state_ref[...] = jnp.zeros(state_ref.shape, F32)

    sel = sel_ref[...]
    dte_ref[...] = _dot01(sel, dt_ref[...], sel_first=False)
    cume_ref[...] = _dot01(sel, cum_ref[...], sel_first=False)

    gwid = SSD_GW
    row = lax.broadcasted_iota(jnp.int32, (CHUNK, gwid), 0)
    lane_pos = lax.broadcasted_iota(jnp.int32, (CHUNK, gwid), 1) & (CHUNK - 1)
    causal = row >= lane_pos
    diag = row == lane_pos
    quad = 4 * SSD_HEADDIM
    blk_r = lax.broadcasted_iota(jnp.int32, (quad, quad), 0) // SSD_HEADDIM
    blk_c = lax.broadcasted_iota(jnp.int32, (quad, quad), 1) // SSD_HEADDIM
    blockdiag = blk_r == blk_c
    d_skip = d_ref[...]
    gw = gw_ref[...]

    def body(c, carry):
        r0 = c * CHUNK
        rows = pl.ds(r0, CHUNK)
        x = x_ref[rows, :]
        cum = cume_ref[rows, :]
        bb = b_ref[rows, :].astype(BF16)
        cb = c_ref[rows, :].astype(BF16)
        xdt = x * dte_ref[rows, :]
        cum_row = jnp.sum(jnp.where(diag, cum, 0.0), axis=0, keepdims=True)
        seg = jnp.where(causal, cum - cum_row, 0.0)
        decay = jnp.where(causal, jnp.exp(seg), 0.0)
        gram = _dot_nt(cb, jnp.concatenate([bb] * SSD_HPG, axis=0))
        attn = (gram * decay).astype(BF16)
        xdt_b = xdt.astype(BF16)
        ys = []
        for qd in range(gwid // quad):
            xq = xdt_b[:, qd * quad:(qd + 1) * quad]
            rhs = jnp.where(blockdiag, jnp.concatenate([xq] * 4, axis=0), jnp.zeros((), BF16))
            ys.append(_dot(attn[:, qd * quad:(qd + 1) * quad], rhs))
        y = jnp.concatenate(ys, axis=1)
        state = state_ref[...]
        y = y + _dot(cb, state.astype(BF16)) * jnp.exp(cum)
        cum_last = cum[CHUNK - 1:CHUNK, :]
        xw = (xdt * jnp.exp(cum_last - cum)).astype(BF16)
        state_ref[...] = state * jnp.exp(cum_last) + _dot_tn(bb, xw)
        y = (y + x * d_skip) * z_ref[rows, :]
        y = y * lax.rsqrt(jnp.mean(y * y, axis=-1, keepdims=True) + NORM_EPS) * gw
        y_ref[rows, :] = y.astype(y_ref.dtype)
        return carry

    for c in range(chunks):
        body(c, 0)


def _ssd_scan(xbc, zs, dt, cum, d_skip, gnorm_w, bsz, tiles_per_batch):
    rows = xbc.shape[0]
    tpb = tiles_per_batch
    b_blk = SSD_DI // SSD_STATE
    c_blk = (SSD_DI + SSD_BC) // SSD_STATE
    d_exp = jnp.repeat(d_skip, SSD_HEADDIM).reshape(1, SSD_DI)
    expand = jnp.asarray(np.arange(SSD_HEADS)[:, None] == (np.arange(SSD_DI)[None, :] // SSD_HEADDIM), BF16)
    wide = pl.BlockSpec((ROW_TILE, SSD_GW), lambda b, g, t: (b * tpb + t, g))
    heads = pl.BlockSpec((ROW_TILE, SSD_HEADS), lambda b, g, t: (b * tpb + t, 0))
    return pl.pallas_call(
        functools.partial(_ssd_scan_kernel, chunks=ROW_TILE // CHUNK),
        grid=(bsz, SSD_GROUPS, tpb),
        in_specs=[wide,
                  pl.BlockSpec((ROW_TILE, SSD_STATE), lambda b, g, t: (b * tpb + t, b_blk + g)),
                  pl.BlockSpec((ROW_TILE, SSD_STATE), lambda b, g, t: (b * tpb + t, c_blk + g)),
                  wide, heads, heads,
                  pl.BlockSpec((SSD_HEADS, SSD_GW), lambda b, g, t: (0, g)),
                  pl.BlockSpec((1, SSD_GW), lambda b, g, t: (0, g)),
                  pl.BlockSpec((1, SSD_GW), lambda b, g, t: (0, g))],
        out_specs=wide,
        out_shape=jax.ShapeDtypeStruct((rows, SSD_DI), BF16),
        scratch_shapes=[pltpu.VMEM((SSD_STATE, SSD_GW), F32),
                        pltpu.VMEM((ROW_TILE, SSD_GW), F32), pltpu.VMEM((ROW_TILE, SSD_GW), F32)],
        compiler_params=_params(("arbitrary", "arbitrary", "arbitrary")),
        name="ssd_scan",
    )(xbc, xbc, xbc, zs, dt, cum, expand, d_exp, gnorm_w.reshape(1, SSD_DI))


def kernel(x, meta_tokens, ret_norm_w, ret_w_in, ret_gn_w, ret_w_out, ssd_norm_w, ssd_w_in, ssd_conv_w, ssd_conv_b, ssd_dt_bias, ssd_a_log, ssd_d, ssd_gnorm_w, ssd_w_out, ffn_norm_w, ffn_w_up, ffn_conv_w, ffn_conv_b, ffn_w_down, final_norm_w):
    bsz, seq, d = x.shape
    length = META_PAD + N_META + seq
    assert d == D_MODEL and length % ROW_TILE == 0 and ROW_TILE % CHUNK == 0
    tpb = length // ROW_TILE
    depth = ffn_w_up.shape[0]

    meta = jnp.broadcast_to(meta_tokens.astype(x.dtype)[None], (bsz, N_META, d))
    h = jnp.concatenate([jnp.zeros((bsz, META_PAD, d), x.dtype), meta, x], axis=1).reshape(bsz * length, d)

    pos_i = np.arange(length) - META_PAD
    vmask = jnp.asarray(np.tile((pos_i >= 0).astype(np.float32), bsz).reshape(bsz * length, 1))
    half = RET_DK // 2
    inv = ROPE_BASE ** (-jnp.arange(half, dtype=F32) / half)
    ang = jnp.asarray(pos_i, F32)[:, None] * inv[None, :]
    cos, sin = jnp.cos(ang), jnp.sin(ang)

    hn = _rmsnorm(h, ret_norm_w[0])
    for i in range(depth):
        j = i // 2
        if i % 2 == 0:
            qk = _qk_proj(hn, ret_w_in, j, cos, sin, tpb)
            v = _proj(hn, ret_w_in, j, 2 * RET_QK, RET_VDIM, BF16, name="ret_v_proj")
            g = _proj(hn, ret_w_in, j, 2 * RET_QK + RET_VDIM, RET_VDIM, F32, name="ret_g_proj")
            y = _ret_scan(qk, v, g, ret_gn_w[j], bsz, tpb)
            h, hn = _out_proj(y, ret_w_out, j, h, vmask, ffn_norm_w[i], BF16, name="ret_out_proj")
        else:
            zs = _proj(hn, ssd_w_in, j, 0, SSD_DI, F32, act="silu", name="ssd_z_proj")
            xbc = _xbc_proj(hn, ssd_w_in, j, ssd_conv_w[j], ssd_conv_b[j], vmask, tpb)
            dt, cum = _dt_path(hn, ssd_w_in, j, ssd_dt_bias[j], ssd_a_log[j])
            y = _ssd_scan(xbc, zs, dt, cum, ssd_d[j], ssd_gnorm_w[j], bsz, tpb)
            h, hn = _out_proj(y, ssd_w_out, j, h, vmask, ffn_norm_w[i], BF16, name="ssd_out_proj")
        act = _ffn_up(hn, ffn_w_up, i, ffn_conv_w[i], ffn_conv_b[i], tpb)
        if i == depth - 1:
            (hn,) = _out_proj(act, ffn_w_down, i, h, vmask, final_norm_w, x.dtype,
                              name="ffn_down_final", keep_residual=False)
        else:
            next_w = ssd_norm_w[(i + 1) // 2] if (i + 1) % 2 == 1 else ret_norm_w[(i + 1) // 2]
            h, hn = _out_proj(act, ffn_w_down, i, h, vmask, next_w, BF16, name="ffn_down_proj")
    return hn.reshape(bsz, length, d)[:, META_PAD + N_META:]
```

```python
import functools

import jax
import jax.numpy as jnp
import numpy as np
from jax import lax
from jax.experimental import pallas as pl
from jax.experimental.pallas import tpu as pltpu

F32 = jnp.float32
BF16 = jnp.bfloat16

D_MODEL = 2048
CHUNK = 64
N_META = 16
META_PAD = CHUNK - N_META
NORM_EPS = 1e-6

RET_HEADS = 8
RET_DK = 256
RET_DV = 512
RET_QK = RET_HEADS * RET_DK
RET_VDIM = RET_HEADS * RET_DV
ROPE_BASE = 10000.0

SSD_DI = 4096
SSD_HEADDIM = 64
SSD_HEADS = 64
SSD_GROUPS = 8
SSD_HPG = 8
SSD_STATE = 128
SSD_BC = SSD_GROUPS * SSD_STATE
SSD_CONV_DIM = SSD_DI + 2 * SSD_BC
SSD_GW = SSD_HPG * SSD_HEADDIM

FFN_DIM = 5632

V7X_VMEM_BYTES = 64 * 1024 * 1024
VMEM_LIMIT = V7X_VMEM_BYTES - 8 * 1024 * 1024

ROW_TILE = 832
COL_TILE = 512
WIDE_COL_TILE = 1024
W_CHUNK = 256
CARRY = 8
MXU_COLS = 256


def _params(sem):
    return pltpu.CompilerParams(dimension_semantics=sem, vmem_limit_bytes=VMEM_LIMIT)


def _sigmoid(x):
    return 1.0 / (1.0 + jnp.exp(-x))


def _silu(x):
    return x * _sigmoid(x)


def _softplus(x):
    return jnp.maximum(x, 0.0) + jnp.log1p(jnp.exp(-jnp.abs(x)))


def _dot(a, b):
    return jnp.dot(a, b, preferred_element_type=F32)


def _dot_nt(a, b):
    return lax.dot_general(a, b, (((1,), (1,)), ((), ())), preferred_element_type=F32)


def _dot_tn(a, b):
    return lax.dot_general(a, b, (((0,), (0,)), ((), ())), preferred_element_type=F32)


def _split3(x):
    hi = x.astype(BF16)
    r1 = x - hi.astype(F32)
    mid = r1.astype(BF16)
    lo = (r1 - mid.astype(F32)).astype(BF16)
    return hi, mid, lo


def _dot01(sel, x, sel_first):
    parts = _split3(x)
    if sel_first:
        return _dot(sel, parts[0]) + _dot(sel, parts[1]) + _dot(sel, parts[2])
    return _dot(parts[0], sel) + _dot(parts[1], sel) + _dot(parts[2], sel)


def _rms_scale(x, w):
    ms = jnp.mean(x * x, axis=-1, keepdims=True)
    return x * lax.rsqrt(ms + NORM_EPS) * w


def _cast_weight(w_ref, wb_ref):
    @pl.when(pl.program_id(1) == 0)
    def _():
        wb_ref[...] = w_ref[...].astype(BF16)


def _rmsnorm_kernel(x_ref, w_ref, o_ref):
    o_ref[...] = _rms_scale(x_ref[...], w_ref[...]).astype(o_ref.dtype)


def _rmsnorm(x, w):
    rows, d = x.shape
    return pl.pallas_call(
        _rmsnorm_kernel,
        grid=(rows // ROW_TILE,),
        in_specs=[pl.BlockSpec((ROW_TILE, d), lambda i: (i, 0)),
                  pl.BlockSpec((1, d), lambda i: (0, 0))],
        out_specs=pl.BlockSpec((ROW_TILE, d), lambda i: (i, 0)),
        out_shape=jax.ShapeDtypeStruct((rows, d), BF16),
        compiler_params=_params(("arbitrary",)),
        name="rmsnorm",
    )(x, w.reshape(1, d))


def _proj_kernel(a_ref, w_ref, o_ref, wb_ref, *, act, w_transposed):
    _cast_weight(w_ref, wb_ref)
    acc = (_dot_nt if w_transposed else _dot)(a_ref[...], wb_ref[...])
    if act == "silu":
        acc = _silu(acc)
    o_ref[...] = acc.astype(o_ref.dtype)


def _weight_spec(k, tn, layer, off, w_transposed):
    if w_transposed:
        return pl.BlockSpec((None, tn, k), lambda j, i: (layer, j + off, 0)), pltpu.VMEM((tn, k), BF16)
    return pl.BlockSpec((None, k, tn), lambda j, i: (layer, 0, j + off)), pltpu.VMEM((k, tn), BF16)


def _proj(a, w, layer, col0, ncols, out_dtype, act=None, name="proj", w_transposed=False):
    rows, k = a.shape
    w_spec, wb_scratch = _weight_spec(k, WIDE_COL_TILE, layer, col0 // WIDE_COL_TILE, w_transposed)
    return pl.pallas_call(
        functools.partial(_proj_kernel, act=act, w_transposed=w_transposed),
        grid=(ncols // WIDE_COL_TILE, rows // ROW_TILE),
        in_specs=[pl.BlockSpec((ROW_TILE, k), lambda j, i: (i, 0)), w_spec],
        out_specs=pl.BlockSpec((ROW_TILE, WIDE_COL_TILE), lambda j, i: (i, j)),
        out_shape=jax.ShapeDtypeStruct((rows, ncols), out_dtype),
        scratch_shapes=[wb_scratch],
        compiler_params=_params(("arbitrary", "arbitrary")),
        name=name,
    )(a, w)


def _qk_kernel(a_ref, w_ref, cos_ref, sin_ref, o_ref, wb_ref, *, nq_tiles):
    _cast_weight(w_ref, wb_ref)
    acc = _dot(a_ref[...], wb_ref[...])
    scale = jnp.where(pl.program_id(0) >= nq_tiles, RET_DK ** -0.5, 1.0).astype(F32)
    cos = cos_ref[...] * scale
    sin = sin_ref[...] * scale
    half = RET_DK // 2
    for hd in range(WIDE_COL_TILE // RET_DK):
        lo = hd * RET_DK
        x1 = acc[:, lo:lo + half]
        x2 = acc[:, lo + half:lo + RET_DK]
        o_ref[:, lo:lo + half] = x1 * cos - x2 * sin
        o_ref[:, lo + half:lo + RET_DK] = x1 * sin + x2 * cos


def _qk_proj(a, w, layer, cos, sin, tiles_per_batch):
    rows, k = a.shape
    half = RET_DK // 2
    return pl.pallas_call(
        functools.partial(_qk_kernel, nq_tiles=RET_QK // WIDE_COL_TILE),
        grid=(2 * RET_QK // WIDE_COL_TILE, rows // ROW_TILE),
        in_specs=[pl.BlockSpec((ROW_TILE, k), lambda j, i: (i, 0)),
                  pl.BlockSpec((None, k, WIDE_COL_TILE), lambda j, i: (layer, 0, j)),
                  pl.BlockSpec((ROW_TILE, half), lambda j, i: (i % tiles_per_batch, 0)),
                  pl.BlockSpec((ROW_TILE, half), lambda j, i: (i % tiles_per_batch, 0))],
        out_specs=pl.BlockSpec((ROW_TILE, WIDE_COL_TILE), lambda j, i: (i, j)),
        out_shape=jax.ShapeDtypeStruct((rows, 2 * RET_QK), F32),
        scratch_shapes=[pltpu.VMEM((k, WIDE_COL_TILE), BF16)],
        compiler_params=_params(("arbitrary", "arbitrary")),
        name="ret_qk_proj",
    )(a, w, cos, sin)


def _causal_conv(buf_ref, cw_ref, cb_ref, rows, width, cols=slice(None)):
    cw = cw_ref[:, cols]
    out = cb_ref[:, cols] + cw[width - 1:width, :] * buf_ref[CARRY:CARRY + rows, cols]
    for tap in range(width - 1):
        shift = width - 1 - tap
        out = out + cw[tap:tap + 1, :] * buf_ref[CARRY - shift:CARRY - shift + rows, cols]
    return out


def _conv_carry_reset(buf_ref, tiles_per_batch):
    @pl.when(pl.program_id(1) % tiles_per_batch == 0)
    def _():
        buf_ref[0:CARRY, :] = jnp.zeros((CARRY, buf_ref.shape[1]), F32)


def _conv_carry_save(buf_ref, rows):
    buf_ref[0:CARRY, :] = buf_ref[rows:rows + CARRY, :]


def _ffn_up_kernel(a_ref, wg_ref, wu_ref, cw_ref, cb_ref, o_ref, wgb_ref, wub_ref, buf_ref,
                   *, tiles_per_batch, width):
    _cast_weight(wg_ref, wgb_ref)
    _cast_weight(wu_ref, wub_ref)
    _conv_carry_reset(buf_ref, tiles_per_batch)
    rows = a_ref.shape[0]
    a = a_ref[...]
    buf_ref[CARRY:CARRY + rows, :] = _dot(a, wgb_ref[...])
    up = _dot(a, wub_ref[...])
    gate = _causal_conv(buf_ref, cw_ref, cb_ref, rows, width)
    o_ref[...] = (_silu(gate) * up).astype(o_ref.dtype)
    _conv_carry_save(buf_ref, rows)


def _ffn_up(a, w_up, layer, conv_w, conv_b, tiles_per_batch):
    rows, k = a.shape
    width = conv_w.shape[0]
    nt = FFN_DIM // COL_TILE
    return pl.pallas_call(
        functools.partial(_ffn_up_kernel, tiles_per_batch=tiles_per_batch, width=width),
        grid=(nt, rows // ROW_TILE),
        in_specs=[pl.BlockSpec((ROW_TILE, k), lambda j, i: (i, 0)),
                  pl.BlockSpec((None, k, COL_TILE), lambda j, i: (layer, 0, j)),
                  pl.BlockSpec((None, k, COL_TILE), lambda j, i: (layer, 0, j + nt)),
                  pl.BlockSpec((width, COL_TILE), lambda j, i: (0, j)),
                  pl.BlockSpec((1, COL_TILE), lambda j, i: (0, j))],
        out_specs=pl.BlockSpec((ROW_TILE, COL_TILE), lambda j, i: (i, j)),
        out_shape=jax.ShapeDtypeStruct((rows, FFN_DIM), BF16),
        scratch_shapes=[pltpu.VMEM((k, COL_TILE), BF16), pltpu.VMEM((k, COL_TILE), BF16),
                        pltpu.VMEM((CARRY + ROW_TILE, COL_TILE), F32)],
        compiler_params=_params(("arbitrary", "arbitrary")),
        name="ffn_up",
    )(a, w_up, w_up, conv_w, conv_b.reshape(1, FFN_DIM))


def _xbc_kernel(a_ref, w_ref, cw_ref, cb_ref, m_ref, o_ref, wb_ref, buf_ref,
                *, tiles_per_batch, width, nx_tiles):
    _cast_weight(w_ref, wb_ref)
    _conv_carry_reset(buf_ref, tiles_per_batch)
    rows = a_ref.shape[0]
    mask = jnp.where(pl.program_id(0) < nx_tiles, m_ref[...], 1.0)
    a = a_ref[...]
    for lo in range(0, o_ref.shape[1], 2 * MXU_COLS):
        cols = slice(lo, lo + 2 * MXU_COLS)
        buf_ref[CARRY:CARRY + rows, cols] = _dot_nt(a, wb_ref[cols, :])
        o_ref[:, cols] = _silu(_causal_conv(buf_ref, cw_ref, cb_ref, rows, width, cols)) * mask
    _conv_carry_save(buf_ref, rows)


def _xbc_proj(a, w_in_t, layer, conv_w, conv_b, vmask, tiles_per_batch):
    rows, k = a.shape
    width = conv_w.shape[0]
    w_spec, wb_scratch = _weight_spec(k, WIDE_COL_TILE, layer, SSD_DI // WIDE_COL_TILE, True)
    return pl.pallas_call(
        functools.partial(_xbc_kernel, tiles_per_batch=tiles_per_batch, width=width,
                          nx_tiles=SSD_DI // WIDE_COL_TILE),
        grid=(SSD_CONV_DIM // WIDE_COL_TILE, rows // ROW_TILE),
        in_specs=[pl.BlockSpec((ROW_TILE, k), lambda j, i: (i, 0)),
                  w_spec,
                  pl.BlockSpec((width, WIDE_COL_TILE), lambda j, i: (0, j)),
                  pl.BlockSpec((1, WIDE_COL_TILE), lambda j, i: (0, j)),
                  pl.BlockSpec((ROW_TILE, 1), lambda j, i: (i, 0))],
        out_specs=pl.BlockSpec((ROW_TILE, WIDE_COL_TILE), lambda j, i: (i, j)),
        out_shape=jax.ShapeDtypeStruct((rows, SSD_CONV_DIM), F32),
        scratch_shapes=[wb_scratch, pltpu.VMEM((CARRY + ROW_TILE, WIDE_COL_TILE), F32)],
        compiler_params=_params(("arbitrary", "arbitrary")),
        name="ssd_xbc_proj",
    )(a, w_in_t, conv_w, conv_b.reshape(1, SSD_CONV_DIM), vmask)


def _dt_kernel(a_ref, w_ref, bias_ref, alog_ref, tri_ref, dt_ref, cum_ref):
    dt = _softplus(_dot_nt(a_ref[...], w_ref[...].astype(BF16)) + bias_ref[...])
    da = dt * (-jnp.exp(alog_ref[...]))
    dt_ref[...] = dt
    cum_ref[...] = _dot01(tri_ref[...], da, sel_first=True)


def _dt_path(a, w_in_t, layer, dt_bias, a_log):
    rows, k = a.shape
    dt_row_blk = (SSD_DI + SSD_CONV_DIM) // SSD_HEADS
    assert dt_row_blk * SSD_HEADS == SSD_DI + SSD_CONV_DIM
    r = np.arange(ROW_TILE)
    tri = ((r[:, None] // CHUNK == r[None, :] // CHUNK) & (r[:, None] >= r[None, :]))
    tri = jnp.asarray(tri, BF16)
    shp = jax.ShapeDtypeStruct((rows, SSD_HEADS), F32)
    return pl.pallas_call(
        _dt_kernel,
        grid=(rows // ROW_TILE,),
        in_specs=[pl.BlockSpec((ROW_TILE, k), lambda i: (i, 0)),
                  pl.BlockSpec((None, SSD_HEADS, k), lambda i: (layer, dt_row_blk, 0)),
                  pl.BlockSpec((1, SSD_HEADS), lambda i: (0, 0)),
                  pl.BlockSpec((1, SSD_HEADS), lambda i: (0, 0)),
                  pl.BlockSpec((ROW_TILE, ROW_TILE), lambda i: (0, 0))],
        out_specs=[pl.BlockSpec((ROW_TILE, SSD_HEADS), lambda i: (i, 0)),
                   pl.BlockSpec((ROW_TILE, SSD_HEADS), lambda i: (i, 0))],
        out_shape=[shp, shp],
        compiler_params=_params(("arbitrary",)),
        name="ssd_dt_path",
    )(a, w_in_t, dt_bias.reshape(1, SSD_HEADS), a_log.reshape(1, SSD_HEADS), tri)


def _out_kernel(a_ref, w_ref, h_ref, m_ref, nw_ref, *refs, n_load, keep_residual):
    if keep_residual:
        ho_ref, hn_ref, wb_ref = refs
    else:
        hn_ref, wb_ref = refs
    step = pl.program_id(0)

    @pl.when(step < n_load)
    def _():
        r0 = pl.multiple_of(step * W_CHUNK, W_CHUNK)
        wb_ref[pl.ds(r0, W_CHUNK), :] = w_ref[...].astype(BF16)

    @pl.when(step >= n_load)
    def _():
        hnew = (h_ref[...] + _dot(a_ref[...], wb_ref[...])) * m_ref[...]
        if keep_residual:
            ho_ref[...] = hnew
        hn_ref[...] = _rms_scale(hnew, nw_ref[...]).astype(hn_ref.dtype)


def _out_row_tile(k):
    return 416 if k <= 4096 else 320


def _out_proj(a, w, layer, h, vmask, norm_w, hn_dtype, name, keep_residual=True):
    rows, k = a.shape
    d = w.shape[2]
    tm = _out_row_tile(k)
    n_load = k // W_CHUNK
    assert n_load * W_CHUNK == k and rows % tm == 0

    def row(step):
        return jnp.maximum(step - n_load, 0)

    row_blk = pl.BlockSpec((tm, d), lambda s: (row(s), 0))
    hn_shape = jax.ShapeDtypeStruct((rows, d), hn_dtype)
    if keep_residual:
        out_specs, out_shape = [row_blk, row_blk], [jax.ShapeDtypeStruct((rows, d), F32), hn_shape]
    else:
        out_specs, out_shape = [row_blk], [hn_shape]
    return pl.pallas_call(
        functools.partial(_out_kernel, n_load=n_load, keep_residual=keep_residual),
        grid=(n_load + rows // tm,),
        in_specs=[pl.BlockSpec((tm, k), lambda s: (row(s), 0)),
                  pl.BlockSpec((None, W_CHUNK, d), lambda s: (layer, jnp.minimum(s, n_load - 1), 0)),
                  row_blk,
                  pl.BlockSpec((tm, 1), lambda s: (row(s), 0)),
                  pl.BlockSpec((1, d), lambda s: (0, 0))],
        out_specs=out_specs,
        out_shape=out_shape,
        scratch_shapes=[pltpu.VMEM((k, d), BF16)],
        compiler_params=_params(("arbitrary",)),
        name=name,
    )(a, w, h, vmask, norm_w.reshape(1, d))


def _ret_scan_kernel(q_ref, k_ref, v_ref, g_ref, gnw_ref, intra_ref, qd_ref, kd_ref, cd_ref,
                     y_ref, state_ref, *, chunks):
    @pl.when(pl.program_id(2) == 0)
    def _():
        state_ref[...] = jnp.zeros(state_ref.shape, F32)

    intra = intra_ref[0]
    q_decay = qd_ref[0]
    k_decay = kd_ref[0]
    chunk_decay = cd_ref[0]
    gnw = gnw_ref[...]

    def body(c, carry):
        r0 = c * CHUNK
        q = q_ref[pl.ds(r0, CHUNK), :]
        k = k_ref[pl.ds(r0, CHUNK), :]
        v = v_ref[pl.ds(r0, CHUNK), :]
        qb = q.astype(BF16)
        scores = _dot_nt(qb, k.astype(BF16)) * intra
        o = _dot(scores.astype(BF16), v)
        state = state_ref[...]
        o = o + q_decay * _dot(qb, state.astype(BF16))
        kd = (k * k_decay).astype(BF16)
        state_ref[...] = state * chunk_decay + _dot_tn(kd, v)
        o = o * lax.rsqrt(jnp.mean(o * o, axis=-1, keepdims=True) + NORM_EPS) * gnw
        y_ref[pl.ds(r0, CHUNK), :] = (_silu(g_ref[pl.ds(r0, CHUNK), :]) * o).astype(y_ref.dtype)
        return carry

    for c in range(chunks):
        body(c, 0)


def _ret_scan(qk, v, g, gn_w, bsz, tiles_per_batch):
    rows = qk.shape[0]
    log_gamma = np.log1p(-np.exp2(-5.0 - np.arange(RET_HEADS, dtype=np.float64)))
    idx = np.arange(CHUNK, dtype=np.float64)
    diff = idx[:, None] - idx[None, :]
    intra = np.where(diff[None] >= 0, np.exp(np.maximum(diff, 0.0)[None] * log_gamma[:, None, None]), 0.0)
    q_decay = np.exp((idx + 1.0)[None, :] * log_gamma[:, None])[..., None]
    k_decay = np.exp((CHUNK - 1.0 - idx)[None, :] * log_gamma[:, None])[..., None]
    chunk_decay = np.exp(CHUNK * log_gamma)[:, None, None]
    intra, q_decay, k_decay, chunk_decay = (jnp.asarray(t, F32) for t in (intra, q_decay, k_decay, chunk_decay))
    tpb = tiles_per_batch
    nkh = RET_QK // RET_DK
    return pl.pallas_call(
        functools.partial(_ret_scan_kernel, chunks=ROW_TILE // CHUNK),
        grid=(bsz, RET_HEADS, tpb),
        in_specs=[pl.BlockSpec((ROW_TILE, RET_DK), lambda b, h, t: (b * tpb + t, h)),
                  pl.BlockSpec((ROW_TILE, RET_DK), lambda b, h, t: (b * tpb + t, nkh + h)),
                  pl.BlockSpec((ROW_TILE, RET_DV), lambda b, h, t: (b * tpb + t, h)),
                  pl.BlockSpec((ROW_TILE, RET_DV), lambda b, h, t: (b * tpb + t, h)),
                  pl.BlockSpec((1, RET_DV), lambda b, h, t: (0, h)),
                  pl.BlockSpec((1, CHUNK, CHUNK), lambda b, h, t: (h, 0, 0)),
                  pl.BlockSpec((1, CHUNK, 1), lambda b, h, t: (h, 0, 0)),
                  pl.BlockSpec((1, CHUNK, 1), lambda b, h, t: (h, 0, 0)),
                  pl.BlockSpec((1, 1, 1), lambda b, h, t: (h, 0, 0))],
        out_specs=pl.BlockSpec((ROW_TILE, RET_DV), lambda b, h, t: (b * tpb + t, h)),
        out_shape=jax.ShapeDtypeStruct((rows, RET_VDIM), BF16),
        scratch_shapes=[pltpu.VMEM((RET_DK, RET_DV), F32)],
        compiler_params=_params(("arbitrary", "arbitrary", "arbitrary")),
        name="ret_scan",
    )(qk, qk, v, g, gn_w.reshape(1, RET_VDIM), intra, q_decay, k_decay, chunk_decay)


def _ssd_scan_kernel(x_ref, b_ref, c_ref, z_ref, dt_ref, cum_ref, sel_ref, d_ref, gw_ref,
                     y_ref, state_ref, dte_ref, cume_ref, *, chunks):
    @pl.when(pl.program_id(2) == 0)
    def _():
        state_ref[...] = jnp.zeros(state_ref.shape, F32)

    sel = sel_ref[...]
    dte_ref[...] = _dot01(sel, dt_ref[...], sel_first=False)
    cume_ref[...] = _dot01(sel, cum_ref[...], sel_first=False)

    gwid = SSD_GW
    row = lax.broadcasted_iota(jnp.int32, (CHUNK, gwid), 0)
    lane_pos = lax.broadcasted_iota(jnp.int32, (CHUNK, gwid), 1) & (CHUNK - 1)
    causal = row >= lane_pos
    diag = row == lane_pos
    quad = 4 * SSD_HEADDIM
    blk_r = lax.broadcasted_iota(jnp.int32, (quad, quad), 0) // SSD_HEADDIM
    blk_c = lax.broadcasted_iota(jnp.int32, (quad, quad), 1) // SSD_HEADDIM
    blockdiag = blk_r == blk_c
    d_skip = d_ref[...]
    gw = gw_ref[...]

    def body(c, carry):
        r0 = c * CHUNK
        rows = pl.ds(r0, CHUNK)
        x = x_ref[rows, :]
        cum = cume_ref[rows, :]
        bb = b_ref[rows, :].astype(BF16)
        cb = c_ref[rows, :].astype(BF16)
        xdt = x * dte_ref[rows, :]
        cum_row = jnp.sum(jnp.where(diag, cum, 0.0), axis=0, keepdims=True)
        decay = jnp.where(causal, jnp.exp(cum - cum_row), 0.0)
        gram = _dot_nt(cb, jnp.concatenate([bb] * SSD_HPG, axis=0))
        attn = (gram * decay).astype(BF16)
        xdt_b = xdt.astype(BF16)
        ys = []
        for qd in range(gwid // quad):
            xq = xdt_b[:, qd * quad:(qd + 1) * quad]
            rhs = jnp.where(blockdiag, jnp.concatenate([xq] * 4, axis=0), jnp.zeros((), BF16))
            ys.append(_dot(attn[:, qd * quad:(qd + 1) * quad], rhs))
        y = jnp.concatenate(ys, axis=1)
        state = state_ref[...]
        y = y + _dot(cb, state.astype(BF16)) * jnp.exp(cum)
        cum_last = cum[CHUNK - 1:CHUNK, :]
        xw = (xdt * jnp.exp(cum_last - cum)).astype(BF16)
        state_ref[...] = state * jnp.exp(cum_last) + _dot_tn(bb, xw)
        y = (y + x * d_skip) * z_ref[rows, :]
        y = y * lax.rsqrt(jnp.mean(y * y, axis=-1, keepdims=True) + NORM_EPS) * gw
        y_ref[rows, :] = y.astype(y_ref.dtype)
        return carry

    for c in range(chunks):
        body(c, 0)


def _ssd_scan(xbc, zs, dt, cum, d_skip, gnorm_w, bsz, tiles_per_batch):
    rows = xbc.shape[0]
    tpb = tiles_per_batch
    b_blk = SSD_DI // SSD_STATE
    c_blk = (SSD_DI + SSD_BC) // SSD_STATE
    d_exp = jnp.repeat(d_skip, SSD_HEADDIM).reshape(1, SSD_DI)
    expand = jnp.asarray(np.arange(SSD_HEADS)[:, None] == (np.arange(SSD_DI)[None, :] // SSD_HEADDIM), BF16)
    wide = pl.BlockSpec((ROW_TILE, SSD_GW), lambda b, g, t: (b * tpb + t, g))
    heads = pl.BlockSpec((ROW_TILE, SSD_HEADS), lambda b, g, t: (b * tpb + t, 0))
    return pl.pallas_call(
        functools.partial(_ssd_scan_kernel, chunks=ROW_TILE // CHUNK),
        grid=(bsz, SSD_GROUPS, tpb),
        in_specs=[wide,
                  pl.BlockSpec((ROW_TILE, SSD_STATE), lambda b, g, t: (b * tpb + t, b_blk + g)),
                  pl.BlockSpec((ROW_TILE, SSD_STATE), lambda b, g, t: (b * tpb + t, c_blk + g)),
                  wide, heads, heads,
                  pl.BlockSpec((SSD_HEADS, SSD_GW), lambda b, g, t: (0, g)),
                  pl.BlockSpec((1, SSD_GW), lambda b, g, t: (0, g)),
                  pl.BlockSpec((1, SSD_GW), lambda b, g, t: (0, g))],
        out_specs=wide,
        out_shape=jax.ShapeDtypeStruct((rows, SSD_DI), BF16),
        scratch_shapes=[pltpu.VMEM((SSD_STATE, SSD_GW), F32),
                        pltpu.VMEM((ROW_TILE, SSD_GW), F32), pltpu.VMEM((ROW_TILE, SSD_GW), F32)],
        compiler_params=_params(("arbitrary", "arbitrary", "arbitrary")),
        name="ssd_scan",
    )(xbc, xbc, xbc, zs, dt, cum, expand, d_exp, gnorm_w.reshape(1, SSD_DI))


def kernel(x, meta_tokens, ret_norm_w, ret_w_in, ret_gn_w, ret_w_out, ssd_norm_w, ssd_w_in, ssd_conv_w, ssd_conv_b, ssd_dt_bias, ssd_a_log, ssd_d, ssd_gnorm_w, ssd_w_out, ffn_norm_w, ffn_w_up, ffn_conv_w, ffn_conv_b, ffn_w_down, final_norm_w):
    bsz, seq, d = x.shape
    length = META_PAD + N_META + seq
    assert d == D_MODEL and length % ROW_TILE == 0 and ROW_TILE % CHUNK == 0
    tpb = length // ROW_TILE
    depth = ffn_w_up.shape[0]

    meta = jnp.broadcast_to(meta_tokens.astype(x.dtype)[None], (bsz, N_META, d))
    h = jnp.concatenate([jnp.zeros((bsz, META_PAD, d), x.dtype), meta, x], axis=1).reshape(bsz * length, d)

    pos_i = np.arange(length) - META_PAD
    vmask = jnp.asarray(np.tile((pos_i >= 0).astype(np.float32), bsz).reshape(bsz * length, 1))
    half = RET_DK // 2
    inv = ROPE_BASE ** (-jnp.arange(half, dtype=F32) / half)
    ang = jnp.asarray(pos_i, F32)[:, None] * inv[None, :]
    cos, sin = jnp.cos(ang), jnp.sin(ang)

    ssd_w_in_t = jnp.swapaxes(ssd_w_in, 1, 2)
    hn = _rmsnorm(h, ret_norm_w[0])
    for i in range(depth):
        j = i // 2
        if i % 2 == 0:
            qk = _qk_proj(hn, ret_w_in, j, cos, sin, tpb)
            v = _proj(hn, ret_w_in, j, 2 * RET_QK, RET_VDIM, BF16, name="ret_v_proj")
            g = _proj(hn, ret_w_in, j, 2 * RET_QK + RET_VDIM, RET_VDIM, F32, name="ret_g_proj")
            y = _ret_scan(qk, v, g, ret_gn_w[j], bsz, tpb)
            h, hn = _out_proj(y, ret_w_out, j, h, vmask, ffn_norm_w[i], BF16, name="ret_out_proj")
        else:
            zs = _proj(hn, ssd_w_in_t, j, 0, SSD_DI, F32, act="silu", name="ssd_z_proj", w_transposed=True)
            xbc = _xbc_proj(hn, ssd_w_in_t, j, ssd_conv_w[j], ssd_conv_b[j], vmask, tpb)
            dt, cum = _dt_path(hn, ssd_w_in_t, j, ssd_dt_bias[j], ssd_a_log[j])
            y = _ssd_scan(xbc, zs, dt, cum, ssd_d[j], ssd_gnorm_w[j], bsz, tpb)
            h, hn = _out_proj(y, ssd_w_out, j, h, vmask, ffn_norm_w[i], BF16, name="ssd_out_proj")
        act = _ffn_up(hn, ffn_w_up, i, ffn_conv_w[i], ffn_conv_b[i], tpb)
        if i == depth - 1:
            (hn,) = _out_proj(act, ffn_w_down, i, h, vmask, final_norm_w, x.dtype,
                              name="ffn_down_final", keep_residual=False)
        else:
            next_w = ssd_norm_w[(i + 1) // 2] if (i + 1) % 2 == 1 else ret_norm_w[(i + 1) // 2]
            h, hn = _out_proj(act, ffn_w_down, i, h, vmask, next_w, BF16, name="ffn_down_proj")
    return hn.reshape(bsz, length, d)[:, META_PAD + N_META:]
```

```python
import functools

import jax
import jax.numpy as jnp
import numpy as np
from jax import lax
from jax.experimental import pallas as pl
from jax.experimental.pallas import tpu as pltpu

F32 = jnp.float32
BF16 = jnp.bfloat16

D_MODEL = 2048
CHUNK = 64
N_META = 16
META_PAD = CHUNK - N_META
NORM_EPS = 1e-6

RET_HEADS = 8
RET_CHUNK = 208
RET_DK = 256
RET_DV = 512
RET_QK = RET_HEADS * RET_DK
RET_VDIM = RET_HEADS * RET_DV
ROPE_BASE = 10000.0

SSD_DI = 4096
SSD_HEADDIM = 64
SSD_HEADS = 64
SSD_GROUPS = 8
SSD_HPG = 8
SSD_STATE = 128
SSD_BC = SSD_GROUPS * SSD_STATE
SSD_CONV_DIM = SSD_DI + 2 * SSD_BC
SSD_GW = SSD_HPG * SSD_HEADDIM

FFN_DIM = 5632

V7X_VMEM_BYTES = 64 * 1024 * 1024
VMEM_LIMIT = V7X_VMEM_BYTES - 8 * 1024 * 1024

ROW_TILE = 832
COL_TILE = 512
WIDE_COL_TILE = 1024
W_CHUNK = 256
CARRY = 8
MXU_COLS = 256


def _params(sem):
    return pltpu.CompilerParams(dimension_semantics=sem, vmem_limit_bytes=VMEM_LIMIT)


def _sigmoid(x):
    return 1.0 / (1.0 + jnp.exp(-x))


def _silu(x):
    return x * _sigmoid(x)


def _softplus(x):
    return jnp.maximum(x, 0.0) + jnp.log1p(jnp.exp(-jnp.abs(x)))


def _dot(a, b):
    return jnp.dot(a, b, preferred_element_type=F32)


def _dot_nt(a, b):
    return lax.dot_general(a, b, (((1,), (1,)), ((), ())), preferred_element_type=F32)


def _dot_tn(a, b):
    return lax.dot_general(a, b, (((0,), (0,)), ((), ())), preferred_element_type=F32)


def _split3(x):
    hi = x.astype(BF16)
    r1 = x - hi.astype(F32)
    mid = r1.astype(BF16)
    lo = (r1 - mid.astype(F32)).astype(BF16)
    return hi, mid, lo


def _dot01(sel, x, sel_first):
    parts = _split3(x)
    if sel_first:
        return _dot(sel, parts[0]) + _dot(sel, parts[1]) + _dot(sel, parts[2])
    return _dot(parts[0], sel) + _dot(parts[1], sel) + _dot(parts[2], sel)


def _rms_scale(x, w):
    ms = jnp.mean(x * x, axis=-1, keepdims=True)
    return x * lax.rsqrt(ms + NORM_EPS) * w


def _cast_weight(w_ref, wb_ref):
    @pl.when(pl.program_id(1) == 0)
    def _():
        wb_ref[...] = w_ref[...].astype(BF16)


def _rmsnorm_kernel(x_ref, w_ref, o_ref):
    o_ref[...] = _rms_scale(x_ref[...], w_ref[...]).astype(o_ref.dtype)


def _rmsnorm(x, w):
    rows, d = x.shape
    return pl.pallas_call(
        _rmsnorm_kernel,
        grid=(rows // ROW_TILE,),
        in_specs=[pl.BlockSpec((ROW_TILE, d), lambda i: (i, 0)),
                  pl.BlockSpec((1, d), lambda i: (0, 0))],
        out_specs=pl.BlockSpec((ROW_TILE, d), lambda i: (i, 0)),
        out_shape=jax.ShapeDtypeStruct((rows, d), BF16),
        compiler_params=_params(("arbitrary",)),
        name="rmsnorm",
    )(x, w.reshape(1, d))


def _proj_kernel(a_ref, w_ref, o_ref, wb_ref, *, act, w_transposed):
    _cast_weight(w_ref, wb_ref)
    acc = (_dot_nt if w_transposed else _dot)(a_ref[...], wb_ref[...])
    if act == "silu":
        acc = _silu(acc)
    o_ref[...] = acc.astype(o_ref.dtype)


def _weight_spec(k, tn, layer, off, w_transposed):
    if w_transposed:
        return pl.BlockSpec((None, tn, k), lambda j, i: (layer, j + off, 0)), pltpu.VMEM((tn, k), BF16)
    return pl.BlockSpec((None, k, tn), lambda j, i: (layer, 0, j + off)), pltpu.VMEM((k, tn), BF16)


def _proj(a, w, layer, col0, ncols, out_dtype, act=None, name="proj", w_transposed=False):
    rows, k = a.shape
    w_spec, wb_scratch = _weight_spec(k, WIDE_COL_TILE, layer, col0 // WIDE_COL_TILE, w_transposed)
    return pl.pallas_call(
        functools.partial(_proj_kernel, act=act, w_transposed=w_transposed),
        grid=(ncols // WIDE_COL_TILE, rows // ROW_TILE),
        in_specs=[pl.BlockSpec((ROW_TILE, k), lambda j, i: (i, 0)), w_spec],
        out_specs=pl.BlockSpec((ROW_TILE, WIDE_COL_TILE), lambda j, i: (i, j)),
        out_shape=jax.ShapeDtypeStruct((rows, ncols), out_dtype),
        scratch_shapes=[wb_scratch],
        compiler_params=_params(("arbitrary", "arbitrary")),
        name=name,
    )(a, w)


def _qk_kernel(a_ref, w_ref, cos_ref, sin_ref, o_ref, wb_ref, *, nq_tiles):
    _cast_weight(w_ref, wb_ref)
    acc = _dot(a_ref[...], wb_ref[...])
    scale = jnp.where(pl.program_id(0) >= nq_tiles, RET_DK ** -0.5, 1.0).astype(F32)
    cos = cos_ref[...] * scale
    sin = sin_ref[...] * scale
    half = RET_DK // 2
    for hd in range(WIDE_COL_TILE // RET_DK):
        lo = hd * RET_DK
        x1 = acc[:, lo:lo + half]
        x2 = acc[:, lo + half:lo + RET_DK]
        o_ref[:, lo:lo + half] = x1 * cos - x2 * sin
        o_ref[:, lo + half:lo + RET_DK] = x1 * sin + x2 * cos


def _qk_proj(a, w, layer, cos, sin, tiles_per_batch):
    rows, k = a.shape
    half = RET_DK // 2
    return pl.pallas_call(
        functools.partial(_qk_kernel, nq_tiles=RET_QK // WIDE_COL_TILE),
        grid=(2 * RET_QK // WIDE_COL_TILE, rows // ROW_TILE),
        in_specs=[pl.BlockSpec((ROW_TILE, k), lambda j, i: (i, 0)),
                  pl.BlockSpec((None, k, WIDE_COL_TILE), lambda j, i: (layer, 0, j)),
                  pl.BlockSpec((ROW_TILE, half), lambda j, i: (i % tiles_per_batch, 0)),
                  pl.BlockSpec((ROW_TILE, half), lambda j, i: (i % tiles_per_batch, 0))],
        out_specs=pl.BlockSpec((ROW_TILE, WIDE_COL_TILE), lambda j, i: (i, j)),
        out_shape=jax.ShapeDtypeStruct((rows, 2 * RET_QK), F32),
        scratch_shapes=[pltpu.VMEM((k, WIDE_COL_TILE), BF16)],
        compiler_params=_params(("arbitrary", "arbitrary")),
        name="ret_qk_proj",
    )(a, w, cos, sin)


def _causal_conv(buf_ref, cw_ref, cb_ref, rows, width, cols=slice(None)):
    cw = cw_ref[:, cols]
    out = cb_ref[:, cols] + cw[width - 1:width, :] * buf_ref[CARRY:CARRY + rows, cols]
    for tap in range(width - 1):
        shift = width - 1 - tap
        out = out + cw[tap:tap + 1, :] * buf_ref[CARRY - shift:CARRY - shift + rows, cols]
    return out


def _conv_carry_reset(buf_ref, tiles_per_batch):
    @pl.when(pl.program_id(1) % tiles_per_batch == 0)
    def _():
        buf_ref[0:CARRY, :] = jnp.zeros((CARRY, buf_ref.shape[1]), F32)


def _conv_carry_save(buf_ref, rows):
    buf_ref[0:CARRY, :] = buf_ref[rows:rows + CARRY, :]


def _ffn_up_kernel(a_ref, wg_ref, wu_ref, cw_ref, cb_ref, o_ref, wgb_ref, wub_ref, buf_ref,
                   *, tiles_per_batch, width):
    _cast_weight(wg_ref, wgb_ref)
    _cast_weight(wu_ref, wub_ref)
    _conv_carry_reset(buf_ref, tiles_per_batch)
    rows = a_ref.shape[0]
    a = a_ref[...]
    buf_ref[CARRY:CARRY + rows, :] = _dot(a, wgb_ref[...])
    up = _dot(a, wub_ref[...])
    gate = _causal_conv(buf_ref, cw_ref, cb_ref, rows, width)
    o_ref[...] = (_silu(gate) * up).astype(o_ref.dtype)
    _conv_carry_save(buf_ref, rows)


def _ffn_up(a, w_up, layer, conv_w, conv_b, tiles_per_batch):
    rows, k = a.shape
    width = conv_w.shape[0]
    nt = FFN_DIM // COL_TILE
    return pl.pallas_call(
        functools.partial(_ffn_up_kernel, tiles_per_batch=tiles_per_batch, width=width),
        grid=(nt, rows // ROW_TILE),
        in_specs=[pl.BlockSpec((ROW_TILE, k), lambda j, i: (i, 0)),
                  pl.BlockSpec((None, k, COL_TILE), lambda j, i: (layer, 0, j)),
                  pl.BlockSpec((None, k, COL_TILE), lambda j, i: (layer, 0, j + nt)),
                  pl.BlockSpec((width, COL_TILE), lambda j, i: (0, j)),
                  pl.BlockSpec((1, COL_TILE), lambda j, i: (0, j))],
        out_specs=pl.BlockSpec((ROW_TILE, COL_TILE), lambda j, i: (i, j)),
        out_shape=jax.ShapeDtypeStruct((rows, FFN_DIM), BF16),
        scratch_shapes=[pltpu.VMEM((k, COL_TILE), BF16), pltpu.VMEM((k, COL_TILE), BF16),
                        pltpu.VMEM((CARRY + ROW_TILE, COL_TILE), F32)],
        compiler_params=_params(("arbitrary", "arbitrary")),
        name="ffn_up",
    )(a, w_up, w_up, conv_w, conv_b.reshape(1, FFN_DIM))


def _xbc_kernel(a_ref, w_ref, cw_ref, cb_ref, m_ref, o_ref, wb_ref, buf_ref,
                *, tiles_per_batch, width, nx_tiles):
    _cast_weight(w_ref, wb_ref)
    _conv_carry_reset(buf_ref, tiles_per_batch)
    rows = a_ref.shape[0]
    mask = jnp.where(pl.program_id(0) < nx_tiles, m_ref[...], 1.0)
    a = a_ref[...]
    for lo in range(0, o_ref.shape[1], 2 * MXU_COLS):
        cols = slice(lo, lo + 2 * MXU_COLS)
        buf_ref[CARRY:CARRY + rows, cols] = _dot_nt(a, wb_ref[cols, :])
        o_ref[:, cols] = _silu(_causal_conv(buf_ref, cw_ref, cb_ref, rows, width, cols)) * mask
    _conv_carry_save(buf_ref, rows)


def _xbc_proj(a, w_in_t, layer, conv_w, conv_b, vmask, tiles_per_batch):
    rows, k = a.shape
    width = conv_w.shape[0]
    w_spec, wb_scratch = _weight_spec(k, WIDE_COL_TILE, layer, SSD_DI // WIDE_COL_TILE, True)
    return pl.pallas_call(
        functools.partial(_xbc_kernel, tiles_per_batch=tiles_per_batch, width=width,
                          nx_tiles=SSD_DI // WIDE_COL_TILE),
        grid=(SSD_CONV_DIM // WIDE_COL_TILE, rows // ROW_TILE),
        in_specs=[pl.BlockSpec((ROW_TILE, k), lambda j, i: (i, 0)),
                  w_spec,
                  pl.BlockSpec((width, WIDE_COL_TILE), lambda j, i: (0, j)),
                  pl.BlockSpec((1, WIDE_COL_TILE), lambda j, i: (0, j)),
                  pl.BlockSpec((ROW_TILE, 1), lambda j, i: (i, 0))],
        out_specs=pl.BlockSpec((ROW_TILE, WIDE_COL_TILE), lambda j, i: (i, j)),
        out_shape=jax.ShapeDtypeStruct((rows, SSD_CONV_DIM), F32),
        scratch_shapes=[wb_scratch, pltpu.VMEM((CARRY + ROW_TILE, WIDE_COL_TILE), F32)],
        compiler_params=_params(("arbitrary", "arbitrary")),
        name="ssd_xbc_proj",
    )(a, w_in_t, conv_w, conv_b.reshape(1, SSD_CONV_DIM), vmask)


def _dt_kernel(a_ref, w_ref, bias_ref, alog_ref, tri_ref, dt_ref, cum_ref):
    dt = _softplus(_dot_nt(a_ref[...], w_ref[...].astype(BF16)) + bias_ref[...])
    da = dt * (-jnp.exp(alog_ref[...]))
    dt_ref[...] = dt
    cum_ref[...] = _dot01(tri_ref[...], da, sel_first=True)


def _dt_path(a, w_in_t, layer, dt_bias, a_log):
    rows, k = a.shape
    dt_row_blk = (SSD_DI + SSD_CONV_DIM) // SSD_HEADS
    assert dt_row_blk * SSD_HEADS == SSD_DI + SSD_CONV_DIM
    r = np.arange(ROW_TILE)
    tri = ((r[:, None] // CHUNK == r[None, :] // CHUNK) & (r[:, None] >= r[None, :]))
    tri = jnp.asarray(tri, BF16)
    shp = jax.ShapeDtypeStruct((rows, SSD_HEADS), F32)
    return pl.pallas_call(
        _dt_kernel,
        grid=(rows // ROW_TILE,),
        in_specs=[pl.BlockSpec((ROW_TILE, k), lambda i: (i, 0)),
                  pl.BlockSpec((None, SSD_HEADS, k), lambda i: (layer, dt_row_blk, 0)),
                  pl.BlockSpec((1, SSD_HEADS), lambda i: (0, 0)),
                  pl.BlockSpec((1, SSD_HEADS), lambda i: (0, 0)),
                  pl.BlockSpec((ROW_TILE, ROW_TILE), lambda i: (0, 0))],
        out_specs=[pl.BlockSpec((ROW_TILE, SSD_HEADS), lambda i: (i, 0)),
                   pl.BlockSpec((ROW_TILE, SSD_HEADS), lambda i: (i, 0))],
        out_shape=[shp, shp],
        compiler_params=_params(("arbitrary",)),
        name="ssd_dt_path",
    )(a, w_in_t, dt_bias.reshape(1, SSD_HEADS), a_log.reshape(1, SSD_HEADS), tri)


def _out_kernel(a_ref, w_ref, h_ref, m_ref, nw_ref, *refs, n_load, keep_residual):
    if keep_residual:
        ho_ref, hn_ref, wb_ref = refs
    else:
        hn_ref, wb_ref = refs
    step = pl.program_id(0)

    @pl.when(step < n_load)
    def _():
        r0 = pl.multiple_of(step * W_CHUNK, W_CHUNK)
        wb_ref[pl.ds(r0, W_CHUNK), :] = w_ref[...].astype(BF16)

    @pl.when(step >= n_load)
    def _():
        hnew = (h_ref[...] + _dot(a_ref[...], wb_ref[...])) * m_ref[...]
        if keep_residual:
            ho_ref[...] = hnew
        hn_ref[...] = _rms_scale(hnew, nw_ref[...]).astype(hn_ref.dtype)


def _out_row_tile(k):
    return 416 if k <= 4096 else 320


def _out_proj(a, w, layer, h, vmask, norm_w, hn_dtype, name, keep_residual=True):
    rows, k = a.shape
    d = w.shape[2]
    tm = _out_row_tile(k)
    n_load = k // W_CHUNK
    assert n_load * W_CHUNK == k and rows % tm == 0

    def row(step):
        return jnp.maximum(step - n_load, 0)

    row_blk = pl.BlockSpec((tm, d), lambda s: (row(s), 0))
    hn_shape = jax.ShapeDtypeStruct((rows, d), hn_dtype)
    if keep_residual:
        out_specs, out_shape = [row_blk, row_blk], [jax.ShapeDtypeStruct((rows, d), F32), hn_shape]
    else:
        out_specs, out_shape = [row_blk], [hn_shape]
    return pl.pallas_call(
        functools.partial(_out_kernel, n_load=n_load, keep_residual=keep_residual),
        grid=(n_load + rows // tm,),
        in_specs=[pl.BlockSpec((tm, k), lambda s: (row(s), 0)),
                  pl.BlockSpec((None, W_CHUNK, d), lambda s: (layer, jnp.minimum(s, n_load - 1), 0)),
                  row_blk,
                  pl.BlockSpec((tm, 1), lambda s: (row(s), 0)),
                  pl.BlockSpec((1, d), lambda s: (0, 0))],
        out_specs=out_specs,
        out_shape=out_shape,
        scratch_shapes=[pltpu.VMEM((k, d), BF16)],
        compiler_params=_params(("arbitrary",)),
        name=name,
    )(a, w, h, vmask, norm_w.reshape(1, d))


def _ret_scan_kernel(q_ref, k_ref, v_ref, gs_ref, gnw_ref, intra_ref, qd_ref, kd_ref, cd_ref,
                     y_ref, state_ref, *, chunks):
    @pl.when(pl.program_id(2) == 0)
    def _():
        state_ref[...] = jnp.zeros(state_ref.shape, F32)

    intra = intra_ref[0]
    q_decay = qd_ref[0]
    k_decay = kd_ref[0]
    chunk_decay = cd_ref[0]
    gnw = gnw_ref[...]

    def body(c, carry):
        r0 = c * RET_CHUNK
        q = q_ref[pl.ds(r0, RET_CHUNK), :]
        k = k_ref[pl.ds(r0, RET_CHUNK), :]
        v = v_ref[pl.ds(r0, RET_CHUNK), :]
        qb = q.astype(BF16)
        scores = _dot_nt(qb, k.astype(BF16)) * intra
        o = _dot(scores.astype(BF16), v)
        state = state_ref[...]
        o = o + q_decay * _dot(qb, state.astype(BF16))
        kd = (k * k_decay).astype(BF16)
        state_ref[...] = state * chunk_decay + _dot_tn(kd, v)
        o = o * lax.rsqrt(jnp.mean(o * o, axis=-1, keepdims=True) + NORM_EPS) * gnw
        y_ref[pl.ds(r0, RET_CHUNK), :] = (gs_ref[pl.ds(r0, RET_CHUNK), :] * o).astype(y_ref.dtype)
        return carry

    for c in range(chunks):
        body(c, 0)


def _ret_scan(qk, v, gs, gn_w, bsz, tiles_per_batch):
    rows = qk.shape[0]
    log_gamma = np.log1p(-np.exp2(-5.0 - np.arange(RET_HEADS, dtype=np.float64)))
    chunk = RET_CHUNK
    assert ROW_TILE % chunk == 0
    idx = np.arange(chunk, dtype=np.float64)
    diff = idx[:, None] - idx[None, :]
    intra = np.where(diff[None] >= 0, np.exp(np.maximum(diff, 0.0)[None] * log_gamma[:, None, None]), 0.0)
    q_decay = np.exp((idx + 1.0)[None, :] * log_gamma[:, None])[..., None]
    k_decay = np.exp((chunk - 1.0 - idx)[None, :] * log_gamma[:, None])[..., None]
    chunk_decay = np.exp(chunk * log_gamma)[:, None, None]
    intra, q_decay, k_decay, chunk_decay = (jnp.asarray(t, F32) for t in (intra, q_decay, k_decay, chunk_decay))
    tpb = tiles_per_batch
    nkh = RET_QK // RET_DK
    return pl.pallas_call(
        functools.partial(_ret_scan_kernel, chunks=ROW_TILE // chunk),
        grid=(bsz, RET_HEADS, tpb),
        in_specs=[pl.BlockSpec((ROW_TILE, RET_DK), lambda b, h, t: (b * tpb + t, h)),
                  pl.BlockSpec((ROW_TILE, RET_DK), lambda b, h, t: (b * tpb + t, nkh + h)),
                  pl.BlockSpec((ROW_TILE, RET_DV), lambda b, h, t: (b * tpb + t, h)),
                  pl.BlockSpec((ROW_TILE, RET_DV), lambda b, h, t: (b * tpb + t, h)),
                  pl.BlockSpec((1, RET_DV), lambda b, h, t: (0, h)),
                  pl.BlockSpec((1, chunk, chunk), lambda b, h, t: (h, 0, 0)),
                  pl.BlockSpec((1, chunk, 1), lambda b, h, t: (h, 0, 0)),
                  pl.BlockSpec((1, chunk, 1), lambda b, h, t: (h, 0, 0)),
                  pl.BlockSpec((1, 1, 1), lambda b, h, t: (h, 0, 0))],
        out_specs=pl.BlockSpec((ROW_TILE, RET_DV), lambda b, h, t: (b * tpb + t, h)),
        out_shape=jax.ShapeDtypeStruct((rows, RET_VDIM), BF16),
        scratch_shapes=[pltpu.VMEM((RET_DK, RET_DV), F32)],
        compiler_params=_params(("arbitrary", "arbitrary", "arbitrary")),
        name="ret_scan",
    )(qk, qk, v, gs, gn_w.reshape(1, RET_VDIM), intra, q_decay, k_decay, chunk_decay)


def _ssd_scan_kernel(x_ref, b_ref, c_ref, z_ref, dt_ref, cum_ref, sel_ref, d_ref, gw_ref,
                     y_ref, state_ref, dte_ref, cume_ref, *, chunks):
    @pl.when(pl.program_id(2) == 0)
    def _():
        state_ref[...] = jnp.zeros(state_ref.shape, F32)

    sel = sel_ref[...]
    dte_ref[...] = _dot01(sel, dt_ref[...], sel_first=False)
    cume_ref[...] = _dot01(sel, cum_ref[...], sel_first=False)

    gwid = SSD_GW
    row = lax.broadcasted_iota(jnp.int32, (CHUNK, gwid), 0)
    lane_pos = lax.broadcasted_iota(jnp.int32, (CHUNK, gwid), 1) & (CHUNK - 1)
    causal = row >= lane_pos
    diag = row == lane_pos
    quad = 4 * SSD_HEADDIM
    blk_r = lax.broadcasted_iota(jnp.int32, (quad, quad), 0) // SSD_HEADDIM
    blk_c = lax.broadcasted_iota(jnp.int32, (quad, quad), 1) // SSD_HEADDIM
    blockdiag = blk_r == blk_c
    d_skip = d_ref[...]
    gw = gw_ref[...]

    def body(c, carry):
        r0 = c * CHUNK
        rows = pl.ds(r0, CHUNK)
        x = x_ref[rows, :]
        cum = cume_ref[rows, :]
        bb = b_ref[rows, :].astype(BF16)
        cb = c_ref[rows, :].astype(BF16)
        xdt = x * dte_ref[rows, :]
        cum_row = jnp.sum(jnp.where(diag, cum, 0.0), axis=0, keepdims=True)
        decay = jnp.where(causal, jnp.exp(cum - cum_row), 0.0)
        gram = _dot_nt(cb, jnp.concatenate([bb] * SSD_HPG, axis=0))
        attn = (gram * decay).astype(BF16)
        xdt_b = xdt.astype(BF16)
        ys = []
        for qd in range(gwid // quad):
            xq = xdt_b[:, qd * quad:(qd + 1) * quad]
            rhs = jnp.where(blockdiag, jnp.concatenate([xq] * 4, axis=0), jnp.zeros((), BF16))
            ys.append(_dot(attn[:, qd * quad:(qd + 1) * quad], rhs))
        y = jnp.concatenate(ys, axis=1)
        state = state_ref[...]
        y = y + _dot(cb, state.astype(BF16)) * jnp.exp(cum)
        cum_last = cum[CHUNK - 1:CHUNK, :]
        xw = (xdt * jnp.exp(cum_last - cum)).astype(BF16)
        state_ref[...] = state * jnp.exp(cum_last) + _dot_tn(bb, xw)
        y = (y + x * d_skip) * z_ref[rows, :]
        y = y * lax.rsqrt(jnp.mean(y * y, axis=-1, keepdims=True) + NORM_EPS) * gw
        y_ref[rows, :] = y.astype(y_ref.dtype)
        return carry

    for c in range(chunks):
        body(c, 0)


def _ssd_scan(xbc, zs, dt, cum, d_skip, gnorm_w, bsz, tiles_per_batch):
    rows = xbc.shape[0]
    tpb = tiles_per_batch
    b_blk = SSD_DI // SSD_STATE
    c_blk = (SSD_DI + SSD_BC) // SSD_STATE
    d_exp = jnp.repeat(d_skip, SSD_HEADDIM).reshape(1, SSD_DI)
    expand = jnp.asarray(np.arange(SSD_HEADS)[:, None] == (np.arange(SSD_DI)[None, :] // SSD_HEADDIM), BF16)
    wide = pl.BlockSpec((ROW_TILE, SSD_GW), lambda b, g, t: (b * tpb + t, g))
    heads = pl.BlockSpec((ROW_TILE, SSD_HEADS), lambda b, g, t: (b * tpb + t, 0))
    return pl.pallas_call(
        functools.partial(_ssd_scan_kernel, chunks=ROW_TILE // CHUNK),
        grid=(bsz, SSD_GROUPS, tpb),
        in_specs=[wide,
                  pl.BlockSpec((ROW_TILE, SSD_STATE), lambda b, g, t: (b * tpb + t, b_blk + g)),
                  pl.BlockSpec((ROW_TILE, SSD_STATE), lambda b, g, t: (b * tpb + t, c_blk + g)),
                  wide, heads, heads,
                  pl.BlockSpec((SSD_HEADS, SSD_GW), lambda b, g, t: (0, g)),
                  pl.BlockSpec((1, SSD_GW), lambda b, g, t: (0, g)),
                  pl.BlockSpec((1, SSD_GW), lambda b, g, t: (0, g))],
        out_specs=wide,
        out_shape=jax.ShapeDtypeStruct((rows, SSD_DI), BF16),
        scratch_shapes=[pltpu.VMEM((SSD_STATE, SSD_GW), F32),
                        pltpu.VMEM((ROW_TILE, SSD_GW), F32), pltpu.VMEM((ROW_TILE, SSD_GW), F32)],
        compiler_params=_params(("arbitrary", "arbitrary", "arbitrary")),
        name="ssd_scan",
    )(xbc, xbc, xbc, zs, dt, cum, expand, d_exp, gnorm_w.reshape(1, SSD_DI))


def kernel(x, meta_tokens, ret_norm_w, ret_w_in, ret_gn_w, ret_w_out, ssd_norm_w, ssd_w_in, ssd_conv_w, ssd_conv_b, ssd_dt_bias, ssd_a_log, ssd_d, ssd_gnorm_w, ssd_w_out, ffn_norm_w, ffn_w_up, ffn_conv_w, ffn_conv_b, ffn_w_down, final_norm_w):
    bsz, seq, d = x.shape
    length = META_PAD + N_META + seq
    assert d == D_MODEL and length % ROW_TILE == 0 and ROW_TILE % CHUNK == 0
    tpb = length // ROW_TILE
    depth = ffn_w_up.shape[0]

    meta = jnp.broadcast_to(meta_tokens.astype(x.dtype)[None], (bsz, N_META, d))
    h = jnp.concatenate([jnp.zeros((bsz, META_PAD, d), x.dtype), meta, x], axis=1).reshape(bsz * length, d)

    pos_i = np.arange(length) - META_PAD
    vmask = jnp.asarray(np.tile((pos_i >= 0).astype(np.float32), bsz).reshape(bsz * length, 1))
    half = RET_DK // 2
    inv = ROPE_BASE ** (-jnp.arange(half, dtype=F32) / half)
    ang = jnp.asarray(pos_i, F32)[:, None] * inv[None, :]
    cos, sin = jnp.cos(ang), jnp.sin(ang)

    ssd_w_in_t = jnp.swapaxes(ssd_w_in, 1, 2)
    hn = _rmsnorm(h, ret_norm_w[0])
    for i in range(depth):
        j = i // 2
        if i % 2 == 0:
            qk = _qk_proj(hn, ret_w_in, j, cos, sin, tpb)
            v = _proj(hn, ret_w_in, j, 2 * RET_QK, RET_VDIM, BF16, name="ret_v_proj")
            gs = _proj(hn, ret_w_in, j, 2 * RET_QK + RET_VDIM, RET_VDIM, F32, act="silu", name="ret_g_proj")
            y = _ret_scan(qk, v, gs, ret_gn_w[j], bsz, tpb)
            h, hn = _out_proj(y, ret_w_out, j, h, vmask, ffn_norm_w[i], BF16, name="ret_out_proj")
        else:
            zs = _proj(hn, ssd_w_in_t, j, 0, SSD_DI, F32, act="silu", name="ssd_z_proj", w_transposed=True)
            xbc = _xbc_proj(hn, ssd_w_in_t, j, ssd_conv_w[j], ssd_conv_b[j], vmask, tpb)
            dt, cum = _dt_path(hn, ssd_w_in_t, j, ssd_dt_bias[j], ssd_a_log[j])
            y = _ssd_scan(xbc, zs, dt, cum, ssd_d[j], ssd_gnorm_w[j], bsz, tpb)
            h, hn = _out_proj(y, ssd_w_out, j, h, vmask, ffn_norm_w[i], BF16, name="ssd_out_proj")
        act = _ffn_up(hn, ffn_w_up, i, ffn_conv_w[i], ffn_conv_b[i], tpb)
        if i == depth - 1:
            (hn,) = _out_proj(act, ffn_w_down, i, h, vmask, final_norm_w, x.dtype,
                              name="ffn_down_final", keep_residual=False)
        else:
            next_w = ssd_norm_w[(i + 1) // 2] if (i + 1) % 2 == 1 else ret_norm_w[(i + 1) // 2]
            h, hn = _out_proj(act, ffn_w_down, i, h, vmask, next_w, BF16, name="ffn_down_proj")
    return hn.reshape(bsz, length, d)[:, META_PAD + N_META:]
```

```python
import functools

import jax
import jax.numpy as jnp
import numpy as np
from jax import lax
from jax.experimental import pallas as pl
from jax.experimental.pallas import tpu as pltpu

F32 = jnp.float32
BF16 = jnp.bfloat16

D_MODEL = 2048
CHUNK = 64
N_META = 16
META_PAD = CHUNK - N_META
NORM_EPS = 1e-6

RET_HEADS = 8
RET_CHUNK = 208
RET_DK = 256
RET_DV = 512
RET_QK = RET_HEADS * RET_DK
RET_VDIM = RET_HEADS * RET_DV
ROPE_BASE = 10000.0

SSD_DI = 4096
SSD_HEADDIM = 64
SSD_HEADS = 64
SSD_GROUPS = 8
SSD_HPG = 8
SSD_STATE = 128
SSD_BC = SSD_GROUPS * SSD_STATE
SSD_CONV_DIM = SSD_DI + 2 * SSD_BC
SSD_GW = SSD_HPG * SSD_HEADDIM
SPLIT_LANES = 256

FFN_DIM = 5632

V7X_VMEM_BYTES = 64 * 1024 * 1024
VMEM_LIMIT = V7X_VMEM_BYTES - 8 * 1024 * 1024

ROW_TILE = 832
PROJ_ROW_TILE = 1040
COL_TILE = 512
WIDE_COL_TILE = 1024
W_CHUNK = 256
CARRY = 8
MXU_COLS = 256


def _params(sem):
    return pltpu.CompilerParams(dimension_semantics=sem, vmem_limit_bytes=VMEM_LIMIT)


def _sigmoid(x):
    return 1.0 / (1.0 + jnp.exp(-x))


def _silu(x):
    return x * _sigmoid(x)


def _softplus(x):
    return jnp.maximum(x, 0.0) + jnp.log1p(jnp.exp(-jnp.abs(x)))


def _dot(a, b):
    return jnp.dot(a, b, preferred_element_type=F32)


def _dot_nt(a, b):
    return lax.dot_general(a, b, (((1,), (1,)), ((), ())), preferred_element_type=F32)


def _dot_tn(a, b):
    return lax.dot_general(a, b, (((0,), (0,)), ((), ())), preferred_element_type=F32)


def _split3(x):
    hi = x.astype(BF16)
    r1 = x - hi.astype(F32)
    mid = r1.astype(BF16)
    lo = (r1 - mid.astype(F32)).astype(BF16)
    return hi, mid, lo


def _dot01(sel, x, sel_first):
    parts = _split3(x)
    if sel_first:
        return _dot(sel, parts[0]) + _dot(sel, parts[1]) + _dot(sel, parts[2])
    return _dot(parts[0], sel) + _dot(parts[1], sel) + _dot(parts[2], sel)


def _rms_scale(x, w):
    ms = jnp.mean(x * x, axis=-1, keepdims=True)
    return x * lax.rsqrt(ms + NORM_EPS) * w


def _cast_weight(w_ref, wb_ref):
    @pl.when(pl.program_id(1) == 0)
    def _():
        wb_ref[...] = w_ref[...].astype(BF16)


def _rmsnorm_kernel(x_ref, w_ref, o_ref):
    o_ref[...] = _rms_scale(x_ref[...], w_ref[...]).astype(o_ref.dtype)


def _rmsnorm(x, w):
    rows, d = x.shape
    return pl.pallas_call(
        _rmsnorm_kernel,
        grid=(rows // ROW_TILE,),
        in_specs=[pl.BlockSpec((ROW_TILE, d), lambda i: (i, 0)),
                  pl.BlockSpec((1, d), lambda i: (0, 0))],
        out_specs=pl.BlockSpec((ROW_TILE, d), lambda i: (i, 0)),
        out_shape=jax.ShapeDtypeStruct((rows, d), BF16),
        compiler_params=_params(("arbitrary",)),
        name="rmsnorm",
    )(x, w.reshape(1, d))


def _proj_kernel(a_ref, w_ref, o_ref, wb_ref, *, act, w_transposed):
    _cast_weight(w_ref, wb_ref)
    acc = (_dot_nt if w_transposed else _dot)(a_ref[...], wb_ref[...])
    if act == "silu":
        acc = _silu(acc)
    o_ref[...] = acc.astype(o_ref.dtype)


def _weight_spec(k, tn, layer, off, w_transposed):
    if w_transposed:
        return pl.BlockSpec((None, tn, k), lambda j, i: (layer, j + off, 0)), pltpu.VMEM((tn, k), BF16)
    return pl.BlockSpec((None, k, tn), lambda j, i: (layer, 0, j + off)), pltpu.VMEM((k, tn), BF16)


def _proj(a, w, layer, col0, ncols, out_dtype, act=None, name="proj", w_transposed=False):
    rows, k = a.shape
    w_spec, wb_scratch = _weight_spec(k, WIDE_COL_TILE, layer, col0 // WIDE_COL_TILE, w_transposed)
    return pl.pallas_call(
        functools.partial(_proj_kernel, act=act, w_transposed=w_transposed),
        grid=(ncols // WIDE_COL_TILE, rows // PROJ_ROW_TILE),
        in_specs=[pl.BlockSpec((PROJ_ROW_TILE, k), lambda j, i: (i, 0)), w_spec],
        out_specs=pl.BlockSpec((PROJ_ROW_TILE, WIDE_COL_TILE), lambda j, i: (i, j)),
        out_shape=jax.ShapeDtypeStruct((rows, ncols), out_dtype),
        scratch_shapes=[wb_scratch],
        compiler_params=_params(("arbitrary", "arbitrary")),
        name=name,
    )(a, w)


def _qk_kernel(a_ref, w_ref, cos_ref, sin_ref, o_ref, wb_ref, *, nq_tiles):
    _cast_weight(w_ref, wb_ref)
    acc = _dot(a_ref[...], wb_ref[...])
    scale = jnp.where(pl.program_id(0) >= nq_tiles, RET_DK ** -0.5, 1.0).astype(F32)
    cos = cos_ref[...] * scale
    sin = sin_ref[...] * scale
    half = RET_DK // 2
    for hd in range(WIDE_COL_TILE // RET_DK):
        lo = hd * RET_DK
        x1 = acc[:, lo:lo + half]
        x2 = acc[:, lo + half:lo + RET_DK]
        o_ref[:, lo:lo + half] = x1 * cos - x2 * sin
        o_ref[:, lo + half:lo + RET_DK] = x1 * sin + x2 * cos


def _qk_proj(a, w, layer, cos, sin, tiles_per_batch):
    rows, k = a.shape
    half = RET_DK // 2
    return pl.pallas_call(
        functools.partial(_qk_kernel, nq_tiles=RET_QK // WIDE_COL_TILE),
        grid=(2 * RET_QK // WIDE_COL_TILE, rows // PROJ_ROW_TILE),
        in_specs=[pl.BlockSpec((PROJ_ROW_TILE, k), lambda j, i: (i, 0)),
                  pl.BlockSpec((None, k, WIDE_COL_TILE), lambda j, i: (layer, 0, j)),
                  pl.BlockSpec((PROJ_ROW_TILE, half), lambda j, i: (i % tiles_per_batch, 0)),
                  pl.BlockSpec((PROJ_ROW_TILE, half), lambda j, i: (i % tiles_per_batch, 0))],
        out_specs=pl.BlockSpec((PROJ_ROW_TILE, WIDE_COL_TILE), lambda j, i: (i, j)),
        out_shape=jax.ShapeDtypeStruct((rows, 2 * RET_QK), F32),
        scratch_shapes=[pltpu.VMEM((k, WIDE_COL_TILE), BF16)],
        compiler_params=_params(("arbitrary", "arbitrary")),
        name="ret_qk_proj",
    )(a, w, cos, sin)


def _causal_conv(buf_ref, cw_ref, cb_ref, rows, width, cols=slice(None)):
    cw = cw_ref[:, cols]
    out = cb_ref[:, cols] + cw[width - 1:width, :] * buf_ref[CARRY:CARRY + rows, cols]
    for tap in range(width - 1):
        shift = width - 1 - tap
        out = out + cw[tap:tap + 1, :] * buf_ref[CARRY - shift:CARRY - shift + rows, cols]
    return out


def _conv_carry_reset(buf_ref, tiles_per_batch):
    @pl.when(pl.program_id(1) % tiles_per_batch == 0)
    def _():
        buf_ref[0:CARRY, :] = jnp.zeros((CARRY, buf_ref.shape[1]), F32)


def _conv_carry_save(buf_ref, rows):
    buf_ref[0:CARRY, :] = buf_ref[rows:rows + CARRY, :]


def _ffn_up_kernel(a_ref, wg_ref, wu_ref, cw_ref, cb_ref, o_ref, wgb_ref, wub_ref, buf_ref,
                   *, tiles_per_batch, width):
    _cast_weight(wg_ref, wgb_ref)
    _cast_weight(wu_ref, wub_ref)
    _conv_carry_reset(buf_ref, tiles_per_batch)
    rows = a_ref.shape[0]
    a = a_ref[...]
    buf_ref[CARRY:CARRY + rows, :] = _dot(a, wgb_ref[...])
    up = _dot(a, wub_ref[...])
    gate = _causal_conv(buf_ref, cw_ref, cb_ref, rows, width)
    o_ref[...] = (_silu(gate) * up).astype(o_ref.dtype)
    _conv_carry_save(buf_ref, rows)


def _ffn_up(a, w_up, layer, conv_w, conv_b, length):
    rows, k = a.shape
    width = conv_w.shape[0]
    nt = FFN_DIM // COL_TILE
    tm = PROJ_ROW_TILE
    assert length % tm == 0
    return pl.pallas_call(
        functools.partial(_ffn_up_kernel, tiles_per_batch=length // tm, width=width),
        grid=(nt, rows // tm),
        in_specs=[pl.BlockSpec((tm, k), lambda j, i: (i, 0)),
                  pl.BlockSpec((None, k, COL_TILE), lambda j, i: (layer, 0, j)),
                  pl.BlockSpec((None, k, COL_TILE), lambda j, i: (layer, 0, j + nt)),
                  pl.BlockSpec((width, COL_TILE), lambda j, i: (0, j)),
                  pl.BlockSpec((1, COL_TILE), lambda j, i: (0, j))],
        out_specs=pl.BlockSpec((tm, COL_TILE), lambda j, i: (i, j)),
        out_shape=jax.ShapeDtypeStruct((rows, FFN_DIM), BF16),
        scratch_shapes=[pltpu.VMEM((k, COL_TILE), BF16), pltpu.VMEM((k, COL_TILE), BF16),
                        pltpu.VMEM((CARRY + tm, COL_TILE), F32)],
        compiler_params=_params(("arbitrary", "arbitrary")),
        name="ffn_up",
    )(a, w_up, w_up, conv_w, conv_b.reshape(1, FFN_DIM))


def _xbc_kernel(a_ref, w_ref, cw_ref, cb_ref, m_ref, o_ref, wb_ref, buf_ref,
                *, tiles_per_batch, width, nx_tiles):
    _cast_weight(w_ref, wb_ref)
    _conv_carry_reset(buf_ref, tiles_per_batch)
    rows = a_ref.shape[0]
    mask = jnp.where(pl.program_id(0) < nx_tiles, m_ref[...], 1.0)
    a = a_ref[...]
    for lo in range(0, o_ref.shape[1], 2 * MXU_COLS):
        cols = slice(lo, lo + 2 * MXU_COLS)
        buf_ref[CARRY:CARRY + rows, cols] = _dot_nt(a, wb_ref[cols, :])
        o_ref[:, cols] = _silu(_causal_conv(buf_ref, cw_ref, cb_ref, rows, width, cols)) * mask
    _conv_carry_save(buf_ref, rows)


def _xbc_proj(a, w_in_t, layer, conv_w, conv_b, vmask, length):
    rows, k = a.shape
    width = conv_w.shape[0]
    tm = PROJ_ROW_TILE
    assert length % tm == 0
    w_spec, wb_scratch = _weight_spec(k, WIDE_COL_TILE, layer, SSD_DI // WIDE_COL_TILE, True)
    return pl.pallas_call(
        functools.partial(_xbc_kernel, tiles_per_batch=length // tm, width=width,
                          nx_tiles=SSD_DI // WIDE_COL_TILE),
        grid=(SSD_CONV_DIM // WIDE_COL_TILE, rows // tm),
        in_specs=[pl.BlockSpec((tm, k), lambda j, i: (i, 0)),
                  w_spec,
                  pl.BlockSpec((width, WIDE_COL_TILE), lambda j, i: (0, j)),
                  pl.BlockSpec((1, WIDE_COL_TILE), lambda j, i: (0, j)),
                  pl.BlockSpec((tm, 1), lambda j, i: (i, 0))],
        out_specs=pl.BlockSpec((tm, WIDE_COL_TILE), lambda j, i: (i, j)),
        out_shape=jax.ShapeDtypeStruct((rows, SSD_CONV_DIM), F32),
        scratch_shapes=[wb_scratch, pltpu.VMEM((CARRY + tm, WIDE_COL_TILE), F32)],
        compiler_params=_params(("arbitrary", "arbitrary")),
        name="ssd_xbc_proj",
    )(a, w_in_t, conv_w, conv_b.reshape(1, SSD_CONV_DIM), vmask)


def _stack3(x, place_ref):
    parts = _split3(x)
    out = _dot(parts[0], place_ref[0]) + _dot(parts[1], place_ref[1]) + _dot(parts[2], place_ref[2])
    return out.astype(BF16)


def _dt_kernel(a_ref, w_ref, bias_ref, alog_ref, tri_ref, place_ref, dt_ref, cum_ref):
    dt = _softplus(_dot_nt(a_ref[...], w_ref[...].astype(BF16)) + bias_ref[...])
    da = dt * (-jnp.exp(alog_ref[...]))
    cum = _dot01(tri_ref[...], da, sel_first=True)
    dt_ref[...] = _stack3(dt, place_ref)
    cum_ref[...] = _stack3(cum, place_ref)


def _dt_path(a, w_in_t, layer, dt_bias, a_log):
    rows, k = a.shape
    dt_row_blk = (SSD_DI + SSD_CONV_DIM) // SSD_HEADS
    assert dt_row_blk * SSD_HEADS == SSD_DI + SSD_CONV_DIM
    r = np.arange(ROW_TILE)
    tri = ((r[:, None] // CHUNK == r[None, :] // CHUNK) & (r[:, None] >= r[None, :]))
    tri = jnp.asarray(tri, BF16)
    place = np.zeros((3, SSD_HEADS, SPLIT_LANES), np.float32)
    for term in range(3):
        place[term, np.arange(SSD_HEADS), term * SSD_HEADS + np.arange(SSD_HEADS)] = 1.0
    place = jnp.asarray(place, BF16)
    shp = jax.ShapeDtypeStruct((rows, SPLIT_LANES), BF16)
    return pl.pallas_call(
        _dt_kernel,
        grid=(rows // ROW_TILE,),
        in_specs=[pl.BlockSpec((ROW_TILE, k), lambda i: (i, 0)),
                  pl.BlockSpec((None, SSD_HEADS, k), lambda i: (layer, dt_row_blk, 0)),
                  pl.BlockSpec((1, SSD_HEADS), lambda i: (0, 0)),
                  pl.BlockSpec((1, SSD_HEADS), lambda i: (0, 0)),
                  pl.BlockSpec((ROW_TILE, ROW_TILE), lambda i: (0, 0)),
                  pl.BlockSpec((3, SSD_HEADS, SPLIT_LANES), lambda i: (0, 0, 0))],
        out_specs=[pl.BlockSpec((ROW_TILE, SPLIT_LANES), lambda i: (i, 0)),
                   pl.BlockSpec((ROW_TILE, SPLIT_LANES), lambda i: (i, 0))],
        out_shape=[shp, shp],
        compiler_params=_params(("arbitrary",)),
        name="ssd_dt_path",
    )(a, w_in_t, dt_bias.reshape(1, SSD_HEADS), a_log.reshape(1, SSD_HEADS), tri, place)


def _out_kernel(a_ref, w_ref, h_ref, m_ref, nw_ref, *refs, n_load, keep_residual):
    if keep_residual:
        ho_ref, hn_ref, wb_ref = refs
    else:
        hn_ref, wb_ref = refs
    step = pl.program_id(0)

    @pl.when(step < n_load)
    def _():
        r0 = pl.multiple_of(step * W_CHUNK, W_CHUNK)
        wb_ref[pl.ds(r0, W_CHUNK), :] = w_ref[...].astype(BF16)

    @pl.when(step >= n_load)
    def _():
        hnew = (h_ref[...] + _dot(a_ref[...], wb_ref[...])) * m_ref[...]
        if keep_residual:
            ho_ref[...] = hnew
        hn_ref[...] = _rms_scale(hnew, nw_ref[...]).astype(hn_ref.dtype)


def _out_row_tile(k):
    return 416 if k <= 4096 else 320


def _out_proj(a, w, layer, h, vmask, norm_w, hn_dtype, name, keep_residual=True):
    rows, k = a.shape
    d = w.shape[2]
    tm = _out_row_tile(k)
    n_load = k // W_CHUNK
    assert n_load * W_CHUNK == k and rows % tm == 0

    def row(step):
        return jnp.maximum(step - n_load, 0)

    row_blk = pl.BlockSpec((tm, d), lambda s: (row(s), 0))
    hn_shape = jax.ShapeDtypeStruct((rows, d), hn_dtype)
    if keep_residual:
        out_specs, out_shape = [row_blk, row_blk], [jax.ShapeDtypeStruct((rows, d), F32), hn_shape]
    else:
        out_specs, out_shape = [row_blk], [hn_shape]
    return pl.pallas_call(
        functools.partial(_out_kernel, n_load=n_load, keep_residual=keep_residual),
        grid=(n_load + rows // tm,),
        in_specs=[pl.BlockSpec((tm, k), lambda s: (row(s), 0)),
                  pl.BlockSpec((None, W_CHUNK, d), lambda s: (layer, jnp.minimum(s, n_load - 1), 0)),
                  row_blk,
                  pl.BlockSpec((tm, 1), lambda s: (row(s), 0)),
                  pl.BlockSpec((1, d), lambda s: (0, 0))],
        out_specs=out_specs,
        out_shape=out_shape,
        scratch_shapes=[pltpu.VMEM((k, d), BF16)],
        compiler_params=_params(("arbitrary",)),
        name=name,
    )(a, w, h, vmask, norm_w.reshape(1, d))


def _ret_scan_kernel(q_ref, k_ref, v_ref, gs_ref, gnw_ref, intra_ref, qd_ref, kd_ref, cd_ref,
                     y_ref, state_ref, *, chunks):
    @pl.when(pl.program_id(2) == 0)
    def _():
        state_ref[...] = jnp.zeros(state_ref.shape, F32)

    intra = intra_ref[0]
    q_decay = qd_ref[0]
    k_decay = kd_ref[0]
    chunk_decay = cd_ref[0]
    gnw = gnw_ref[...]

    def body(c, carry):
        r0 = c * RET_CHUNK
        q = q_ref[pl.ds(r0, RET_CHUNK), :]
        k = k_ref[pl.ds(r0, RET_CHUNK), :]
        v = v_ref[pl.ds(r0, RET_CHUNK), :]
        qb = q.astype(BF16)
        scores = _dot_nt(qb, k.astype(BF16)) * intra
        o = _dot(scores.astype(BF16), v)
        state = state_ref[...]
        o = o + q_decay * _dot(qb, state.astype(BF16))
        kd = (k * k_decay).astype(BF16)
        state_ref[...] = state * chunk_decay + _dot_tn(kd, v)
        o = o * lax.rsqrt(jnp.mean(o * o, axis=-1, keepdims=True) + NORM_EPS) * gnw
        y_ref[pl.ds(r0, RET_CHUNK), :] = (gs_ref[pl.ds(r0, RET_CHUNK), :] * o).astype(y_ref.dtype)
        return carry

    for c in range(chunks):
        body(c, 0)


def _ret_scan(qk, v, gs, gn_w, bsz, tiles_per_batch):
    rows = qk.shape[0]
    log_gamma = np.log1p(-np.exp2(-5.0 - np.arange(RET_HEADS, dtype=np.float64)))
    chunk = RET_CHUNK
    assert ROW_TILE % chunk == 0
    idx = np.arange(chunk, dtype=np.float64)
    diff = idx[:, None] - idx[None, :]
    intra = np.where(diff[None] >= 0, np.exp(np.maximum(diff, 0.0)[None] * log_gamma[:, None, None]), 0.0)
    q_decay = np.exp((idx + 1.0)[None, :] * log_gamma[:, None])[..., None]
    k_decay = np.exp((chunk - 1.0 - idx)[None, :] * log_gamma[:, None])[..., None]
    chunk_decay = np.exp(chunk * log_gamma)[:, None, None]
    intra, q_decay, k_decay, chunk_decay = (jnp.asarray(t, F32) for t in (intra, q_decay, k_decay, chunk_decay))
    tpb = tiles_per_batch
    nkh = RET_QK // RET_DK
    return pl.pallas_call(
        functools.partial(_ret_scan_kernel, chunks=ROW_TILE // chunk),
        grid=(bsz, RET_HEADS, tpb),
        in_specs=[pl.BlockSpec((ROW_TILE, RET_DK), lambda b, h, t: (b * tpb + t, h)),
                  pl.BlockSpec((ROW_TILE, RET_DK), lambda b, h, t: (b * tpb + t, nkh + h)),
                  pl.BlockSpec((ROW_TILE, RET_DV), lambda b, h, t: (b * tpb + t, h)),
                  pl.BlockSpec((ROW_TILE, RET_DV), lambda b, h, t: (b * tpb + t, h)),
                  pl.BlockSpec((1, RET_DV), lambda b, h, t: (0, h)),
                  pl.BlockSpec((1, chunk, chunk), lambda b, h, t: (h, 0, 0)),
                  pl.BlockSpec((1, chunk, 1), lambda b, h, t: (h, 0, 0)),
                  pl.BlockSpec((1, chunk, 1), lambda b, h, t: (h, 0, 0)),
                  pl.BlockSpec((1, 1, 1), lambda b, h, t: (h, 0, 0))],
        out_specs=pl.BlockSpec((ROW_TILE, RET_DV), lambda b, h, t: (b * tpb + t, h)),
        out_shape=jax.ShapeDtypeStruct((rows, RET_VDIM), BF16),
        scratch_shapes=[pltpu.VMEM((RET_DK, RET_DV), F32)],
        compiler_params=_params(("arbitrary", "arbitrary", "arbitrary")),
        name="ret_scan",
    )(qk, qk, v, gs, gn_w.reshape(1, RET_VDIM), intra, q_decay, k_decay, chunk_decay)


def _ssd_scan_kernel(x_ref, b_ref, c_ref, z_ref, dt_ref, cum_ref, sel_ref, d_ref, gw_ref,
                     y_ref, state_ref, dte_ref, cume_ref, *, chunks):
    @pl.when(pl.program_id(2) == 0)
    def _():
        state_ref[...] = jnp.zeros(state_ref.shape, F32)

    sel = sel_ref[...]
    dte_ref[...] = _dot(dt_ref[...], sel)
    cume_ref[...] = _dot(cum_ref[...], sel)

    gwid = SSD_GW
    row = lax.broadcasted_iota(jnp.int32, (CHUNK, gwid), 0)
    lane_pos = lax.broadcasted_iota(jnp.int32, (CHUNK, gwid), 1) & (CHUNK - 1)
    causal = row >= lane_pos
    diag = row == lane_pos
    quad = 4 * SSD_HEADDIM
    blk_r = lax.broadcasted_iota(jnp.int32, (quad, quad), 0) // SSD_HEADDIM
    blk_c = lax.broadcasted_iota(jnp.int32, (quad, quad), 1) // SSD_HEADDIM
    blockdiag = blk_r == blk_c
    d_skip = d_ref[...]
    gw = gw_ref[...]

    def body(c, carry):
        r0 = c * CHUNK
        rows = pl.ds(r0, CHUNK)
        x = x_ref[rows, :]
        cum = cume_ref[rows, :]
        bb = b_ref[rows, :].astype(BF16)
        cb = c_ref[rows, :].astype(BF16)
        xdt = x * dte_ref[rows, :]
        cum_row = jnp.sum(jnp.where(diag, cum, 0.0), axis=0, keepdims=True)
        decay = jnp.where(causal, jnp.exp(cum - cum_row), 0.0)
        gram = _dot_nt(cb, jnp.concatenate([bb] * SSD_HPG, axis=0))
        attn = (gram * decay).astype(BF16)
        xdt_b = xdt.astype(BF16)
        ys = []
        for qd in range(gwid // quad):
            xq = xdt_b[:, qd * quad:(qd + 1) * quad]
            rhs = jnp.where(blockdiag, jnp.concatenate([xq] * 4, axis=0), jnp.zeros((), BF16))
            ys.append(_dot(attn[:, qd * quad:(qd + 1) * quad], rhs))
        y = jnp.concatenate(ys, axis=1)
        state = state_ref[...]
        y = y + _dot(cb, state.astype(BF16)) * jnp.exp(cum)
        cum_last = cum[CHUNK - 1:CHUNK, :]
        xw = (xdt * jnp.exp(cum_last - cum)).astype(BF16)
        state_ref[...] = state * jnp.exp(cum_last) + _dot_tn(bb, xw)
        y = (y + x * d_skip) * z_ref[rows, :]
        y = y * lax.rsqrt(jnp.mean(y * y, axis=-1, keepdims=True) + NORM_EPS) * gw
        y_ref[rows, :] = y.astype(y_ref.dtype)
        return carry

    for c in range(chunks):
        body(c, 0)


def _ssd_scan(xbc, zs, dt, cum, d_skip, gnorm_w, bsz, tiles_per_batch):
    rows = xbc.shape[0]
    tpb = tiles_per_batch
    b_blk = SSD_DI // SSD_STATE
    c_blk = (SSD_DI + SSD_BC) // SSD_STATE
    d_exp = jnp.repeat(d_skip, SSD_HEADDIM).reshape(1, SSD_DI)
    one_hot = np.arange(SSD_HEADS)[:, None] == (np.arange(SSD_DI)[None, :] // SSD_HEADDIM)
    expand = np.zeros((SPLIT_LANES, SSD_DI), np.float32)
    expand[:3 * SSD_HEADS] = np.tile(one_hot, (3, 1))
    expand = jnp.asarray(expand, BF16)
    wide = pl.BlockSpec((ROW_TILE, SSD_GW), lambda b, g, t: (b * tpb + t, g))
    heads = pl.BlockSpec((ROW_TILE, SPLIT_LANES), lambda b, g, t: (b * tpb + t, 0))
    return pl.pallas_call(
        functools.partial(_ssd_scan_kernel, chunks=ROW_TILE // CHUNK),
        grid=(bsz, SSD_GROUPS, tpb),
        in_specs=[wide,
                  pl.BlockSpec((ROW_TILE, SSD_STATE), lambda b, g, t: (b * tpb + t, b_blk + g)),
                  pl.BlockSpec((ROW_TILE, SSD_STATE), lambda b, g, t: (b * tpb + t, c_blk + g)),
                  wide, heads, heads,
                  pl.BlockSpec((SPLIT_LANES, SSD_GW), lambda b, g, t: (0, g)),
                  pl.BlockSpec((1, SSD_GW), lambda b, g, t: (0, g)),
                  pl.BlockSpec((1, SSD_GW), lambda b, g, t: (0, g))],
        out_specs=wide,
        out_shape=jax.ShapeDtypeStruct((rows, SSD_DI), BF16),
        scratch_shapes=[pltpu.VMEM((SSD_STATE, SSD_GW), F32),
                        pltpu.VMEM((ROW_TILE, SSD_GW), F32), pltpu.VMEM((ROW_TILE, SSD_GW), F32)],
        compiler_params=_params(("arbitrary", "arbitrary", "arbitrary")),
        name="ssd_scan",
    )(xbc, xbc, xbc, zs, dt, cum, expand, d_exp, gnorm_w.reshape(1, SSD_DI))


def kernel(x, meta_tokens, ret_norm_w, ret_w_in, ret_gn_w, ret_w_out, ssd_norm_w, ssd_w_in, ssd_conv_w, ssd_conv_b, ssd_dt_bias, ssd_a_log, ssd_d, ssd_gnorm_w, ssd_w_out, ffn_norm_w, ffn_w_up, ffn_conv_w, ffn_conv_b, ffn_w_down, final_norm_w):
    bsz, seq, d = x.shape
    length = META_PAD + N_META + seq
    assert d == D_MODEL and length % ROW_TILE == 0 and ROW_TILE % CHUNK == 0 and length % PROJ_ROW_TILE == 0
    tpb = length // ROW_TILE
    depth = ffn_w_up.shape[0]

    meta = jnp.broadcast_to(meta_tokens.astype(x.dtype)[None], (bsz, N_META, d))
    h = jnp.concatenate([jnp.zeros((bsz, META_PAD, d), x.dtype), meta, x], axis=1).reshape(bsz * length, d)

    pos_i = np.arange(length) - META_PAD
    vmask = jnp.asarray(np.tile((pos_i >= 0).astype(np.float32), bsz).reshape(bsz * length, 1))
    half = RET_DK // 2
    inv = ROPE_BASE ** (-jnp.arange(half, dtype=F32) / half)
    ang = jnp.asarray(pos_i, F32)[:, None] * inv[None, :]
    cos, sin = jnp.cos(ang), jnp.sin(ang)

    ssd_w_in_t = jnp.swapaxes(ssd_w_in, 1, 2)
    hn = _rmsnorm(h, ret_norm_w[0])
    for i in range(depth):
        j = i // 2
        if i % 2 == 0:
            qk = _qk_proj(hn, ret_w_in, j, cos, sin, length // PROJ_ROW_TILE)
            v = _proj(hn, ret_w_in, j, 2 * RET_QK, RET_VDIM, BF16, name="ret_v_proj")
            gs = _proj(hn, ret_w_in, j, 2 * RET_QK + RET_VDIM, RET_VDIM, F32, act="silu", name="ret_g_proj")
            y = _ret_scan(qk, v, gs, ret_gn_w[j], bsz, tpb)
            h, hn = _out_proj(y, ret_w_out, j, h, vmask, ffn_norm_w[i], BF16, name="ret_out_proj")
        else:
            zs = _proj(hn, ssd_w_in_t, j, 0, SSD_DI, F32, act="silu", name="ssd_z_proj", w_transposed=True)
            xbc = _xbc_proj(hn, ssd_w_in_t, j, ssd_conv_w[j], ssd_conv_b[j], vmask, length)
            dt, cum = _dt_path(hn, ssd_w_in_t, j, ssd_dt_bias[j], ssd_a_log[j])
            y = _ssd_scan(xbc, zs, dt, cum, ssd_d[j], ssd_gnorm_w[j], bsz, tpb)
            h, hn = _out_proj(y, ssd_w_out, j, h, vmask, ffn_norm_w[i], BF16, name="ssd_out_proj")
        act = _ffn_up(hn, ffn_w_up, i, ffn_conv_w[i], ffn_conv_b[i], length)
        if i == depth - 1:
            (hn,) = _out_proj(act, ffn_w_down, i, h, vmask, final_norm_w, x.dtype,
                              name="ffn_down_final", keep_residual=False)
        else:
            next_w = ssd_norm_w[(i + 1) // 2] if (i + 1) % 2 == 1 else ret_norm_w[(i + 1) // 2]
            h, hn = _out_proj(act, ffn_w_down, i, h, vmask, next_w, BF16, name="ffn_down_proj")
    return hn.reshape(bsz, length, d)[:, META_PAD + N_META:]
```

```python
import functools

import jax
import jax.numpy as jnp
import numpy as np
from jax import lax
from jax.experimental import pallas as pl
from jax.experimental.pallas import tpu as pltpu

F32 = jnp.float32
BF16 = jnp.bfloat16

D_MODEL = 2048
CHUNK = 64
N_META = 16
META_PAD = CHUNK - N_META
NORM_EPS = 1e-6

RET_HEADS = 8
RET_CHUNK = 208
RET_DK = 256
RET_DV = 512
RET_QK = RET_HEADS * RET_DK
RET_VDIM = RET_HEADS * RET_DV
ROPE_BASE = 10000.0

SSD_DI = 4096
SSD_HEADDIM = 64
SSD_HEADS = 64
SSD_GROUPS = 8
SSD_HPG = 8
SSD_STATE = 128
SSD_BC = SSD_GROUPS * SSD_STATE
SSD_CONV_DIM = SSD_DI + 2 * SSD_BC
SSD_GW = SSD_HPG * SSD_HEADDIM
SPLIT_LANES = 256

FFN_DIM = 5632

V7X_VMEM_BYTES = 64 * 1024 * 1024
VMEM_LIMIT = V7X_VMEM_BYTES - 8 * 1024 * 1024

ROW_TILE = 832
PROJ_ROW_TILE = 1040
COL_TILE = 512
WIDE_COL_TILE = 1024
W_CHUNK = 256
CARRY = 8
MXU_COLS = 256


def _params(sem):
    return pltpu.CompilerParams(dimension_semantics=sem, vmem_limit_bytes=VMEM_LIMIT)


def _sigmoid(x):
    return 1.0 / (1.0 + jnp.exp(-x))


def _silu(x):
    return x * _sigmoid(x)


def _softplus(x):
    return jnp.maximum(x, 0.0) + jnp.log1p(jnp.exp(-jnp.abs(x)))


def _dot(a, b):
    return jnp.dot(a, b, preferred_element_type=F32)


def _dot_nt(a, b):
    return lax.dot_general(a, b, (((1,), (1,)), ((), ())), preferred_element_type=F32)


def _dot_tn(a, b):
    return lax.dot_general(a, b, (((0,), (0,)), ((), ())), preferred_element_type=F32)


def _split3(x):
    hi = x.astype(BF16)
    r1 = x - hi.astype(F32)
    mid = r1.astype(BF16)
    lo = (r1 - mid.astype(F32)).astype(BF16)
    return hi, mid, lo


def _dot01(sel, x, sel_first):
    parts = _split3(x)
    if sel_first:
        return _dot(sel, parts[0]) + _dot(sel, parts[1]) + _dot(sel, parts[2])
    return _dot(parts[0], sel) + _dot(parts[1], sel) + _dot(parts[2], sel)


def _rms_scale(x, w):
    ms = jnp.mean(x * x, axis=-1, keepdims=True)
    return x * lax.rsqrt(ms + NORM_EPS) * w


def _cast_weight(w_ref, wb_ref):
    @pl.when(pl.program_id(1) == 0)
    def _():
        wb_ref[...] = w_ref[...].astype(BF16)


def _rmsnorm_kernel(x_ref, w_ref, o_ref):
    o_ref[...] = _rms_scale(x_ref[...], w_ref[...]).astype(o_ref.dtype)


def _rmsnorm(x, w):
    rows, d = x.shape
    return pl.pallas_call(
        _rmsnorm_kernel,
        grid=(rows // ROW_TILE,),
        in_specs=[pl.BlockSpec((ROW_TILE, d), lambda i: (i, 0)),
                  pl.BlockSpec((1, d), lambda i: (0, 0))],
        out_specs=pl.BlockSpec((ROW_TILE, d), lambda i: (i, 0)),
        out_shape=jax.ShapeDtypeStruct((rows, d), BF16),
        compiler_params=_params(("arbitrary",)),
        name="rmsnorm",
    )(x, w.reshape(1, d))


def _proj_kernel(a_ref, w_ref, o_ref, wb_ref, *, act):
    _cast_weight(w_ref, wb_ref)
    acc = _dot(a_ref[...], wb_ref[...])
    if act == "silu":
        acc = _silu(acc)
    o_ref[...] = acc.astype(o_ref.dtype)


def _proj(a, w, layer, col0, ncols, out_dtype, act=None, name="proj"):
    rows, k = a.shape
    off = col0 // WIDE_COL_TILE
    return pl.pallas_call(
        functools.partial(_proj_kernel, act=act),
        grid=(ncols // WIDE_COL_TILE, rows // PROJ_ROW_TILE),
        in_specs=[pl.BlockSpec((PROJ_ROW_TILE, k), lambda j, i: (i, 0)),
                  pl.BlockSpec((None, k, WIDE_COL_TILE), lambda j, i: (layer, 0, j + off))],
        out_specs=pl.BlockSpec((PROJ_ROW_TILE, WIDE_COL_TILE), lambda j, i: (i, j)),
        out_shape=jax.ShapeDtypeStruct((rows, ncols), out_dtype),
        scratch_shapes=[pltpu.VMEM((k, WIDE_COL_TILE), BF16)],
        compiler_params=_params(("arbitrary", "arbitrary")),
        name=name,
    )(a, w)


def _qk_kernel(a_ref, w_ref, cos_ref, sin_ref, o_ref, wb_ref, *, nq_tiles):
    _cast_weight(w_ref, wb_ref)
    acc = _dot(a_ref[...], wb_ref[...])
    scale = jnp.where(pl.program_id(0) >= nq_tiles, RET_DK ** -0.5, 1.0).astype(F32)
    cos = cos_ref[...] * scale
    sin = sin_ref[...] * scale
    half = RET_DK // 2
    for hd in range(WIDE_COL_TILE // RET_DK):
        lo = hd * RET_DK
        x1 = acc[:, lo:lo + half]
        x2 = acc[:, lo + half:lo + RET_DK]
        o_ref[:, lo:lo + half] = x1 * cos - x2 * sin
        o_ref[:, lo + half:lo + RET_DK] = x1 * sin + x2 * cos


def _qk_proj(a, w, layer, cos, sin, tiles_per_batch):
    rows, k = a.shape
    half = RET_DK // 2
    return pl.pallas_call(
        functools.partial(_qk_kernel, nq_tiles=RET_QK // WIDE_COL_TILE),
        grid=(2 * RET_QK // WIDE_COL_TILE, rows // PROJ_ROW_TILE),
        in_specs=[pl.BlockSpec((PROJ_ROW_TILE, k), lambda j, i: (i, 0)),
                  pl.BlockSpec((None, k, WIDE_COL_TILE), lambda j, i: (layer, 0, j)),
                  pl.BlockSpec((PROJ_ROW_TILE, half), lambda j, i: (i % tiles_per_batch, 0)),
                  pl.BlockSpec((PROJ_ROW_TILE, half), lambda j, i: (i % tiles_per_batch, 0))],
        out_specs=pl.BlockSpec((PROJ_ROW_TILE, WIDE_COL_TILE), lambda j, i: (i, j)),
        out_shape=jax.ShapeDtypeStruct((rows, 2 * RET_QK), F32),
        scratch_shapes=[pltpu.VMEM((k, WIDE_COL_TILE), BF16)],
        compiler_params=_params(("arbitrary", "arbitrary")),
        name="ret_qk_proj",
    )(a, w, cos, sin)


def _causal_conv(buf_ref, cw_ref, cb_ref, rows, width, cols=slice(None)):
    cw = cw_ref[:, cols]
    out = cb_ref[:, cols] + cw[width - 1:width, :] * buf_ref[CARRY:CARRY + rows, cols]
    for tap in range(width - 1):
        shift = width - 1 - tap
        out = out + cw[tap:tap + 1, :] * buf_ref[CARRY - shift:CARRY - shift + rows, cols]
    return out


def _conv_carry_reset(buf_ref, tiles_per_batch):
    @pl.when(pl.program_id(1) % tiles_per_batch == 0)
    def _():
        buf_ref[0:CARRY, :] = jnp.zeros((CARRY, buf_ref.shape[1]), F32)


def _conv_carry_save(buf_ref, rows):
    buf_ref[0:CARRY, :] = buf_ref[rows:rows + CARRY, :]


def _ffn_up_kernel(a_ref, wg_ref, wu_ref, cw_ref, cb_ref, o_ref, wgb_ref, wub_ref, buf_ref,
                   *, tiles_per_batch, width):
    _cast_weight(wg_ref, wgb_ref)
    _cast_weight(wu_ref, wub_ref)
    _conv_carry_reset(buf_ref, tiles_per_batch)
    rows = a_ref.shape[0]
    a = a_ref[...]
    buf_ref[CARRY:CARRY + rows, :] = _dot(a, wgb_ref[...])
    up = _dot(a, wub_ref[...])
    gate = _causal_conv(buf_ref, cw_ref, cb_ref, rows, width)
    o_ref[...] = (_silu(gate) * up).astype(o_ref.dtype)
    _conv_carry_save(buf_ref, rows)


def _ffn_up(a, w_up, layer, conv_w, conv_b, length):
    rows, k = a.shape
    width = conv_w.shape[0]
    nt = FFN_DIM // COL_TILE
    tm = PROJ_ROW_TILE
    assert length % tm == 0
    return pl.pallas_call(
        functools.partial(_ffn_up_kernel, tiles_per_batch=length // tm, width=width),
        grid=(nt, rows // tm),
        in_specs=[pl.BlockSpec((tm, k), lambda j, i: (i, 0)),
                  pl.BlockSpec((None, k, COL_TILE), lambda j, i: (layer, 0, j)),
                  pl.BlockSpec((None, k, COL_TILE), lambda j, i: (layer, 0, j + nt)),
                  pl.BlockSpec((width, COL_TILE), lambda j, i: (0, j)),
                  pl.BlockSpec((1, COL_TILE), lambda j, i: (0, j))],
        out_specs=pl.BlockSpec((tm, COL_TILE), lambda j, i: (i, j)),
        out_shape=jax.ShapeDtypeStruct((rows, FFN_DIM), BF16),
        scratch_shapes=[pltpu.VMEM((k, COL_TILE), BF16), pltpu.VMEM((k, COL_TILE), BF16),
                        pltpu.VMEM((CARRY + tm, COL_TILE), F32)],
        compiler_params=_params(("arbitrary", "arbitrary")),
        name="ffn_up",
    )(a, w_up, w_up, conv_w, conv_b.reshape(1, FFN_DIM))


def _xz_kernel(a_ref, wx_ref, wz_ref, cw_ref, cb_ref, m_ref, xo_ref, zo_ref, wxb_ref, wzb_ref, buf_ref,
               *, tiles_per_batch, width):
    _cast_weight(wx_ref, wxb_ref)
    _cast_weight(wz_ref, wzb_ref)
    _conv_carry_reset(buf_ref, tiles_per_batch)
    rows = a_ref.shape[0]
    a = a_ref[...]
    buf_ref[CARRY:CARRY + rows, :] = _dot_nt(a, wxb_ref[...])
    z = _dot_nt(a, wzb_ref[...])
    xo_ref[...] = _silu(_causal_conv(buf_ref, cw_ref, cb_ref, rows, width)) * m_ref[...]
    zo_ref[...] = _silu(z)
    _conv_carry_save(buf_ref, rows)


def _xz_proj(a, w_in_t, layer, conv_w, conv_b, vmask, length):
    rows, k = a.shape
    width = conv_w.shape[0]
    tm, tn = PROJ_ROW_TILE, COL_TILE
    assert length % tm == 0
    x_off = SSD_DI // tn
    w_blk = lambda off: pl.BlockSpec((None, tn, k), lambda j, i: (layer, j + off, 0))
    out_blk = pl.BlockSpec((tm, tn), lambda j, i: (i, j))
    shp = jax.ShapeDtypeStruct((rows, SSD_DI), F32)
    return pl.pallas_call(
        functools.partial(_xz_kernel, tiles_per_batch=length // tm, width=width),
        grid=(SSD_DI // tn, rows // tm),
        in_specs=[pl.BlockSpec((tm, k), lambda j, i: (i, 0)),
                  w_blk(x_off), w_blk(0),
                  pl.BlockSpec((width, tn), lambda j, i: (0, j)),
                  pl.BlockSpec((1, tn), lambda j, i: (0, j)),
                  pl.BlockSpec((tm, 1), lambda j, i: (i, 0))],
        out_specs=[out_blk, out_blk],
        out_shape=[shp, shp],
        scratch_shapes=[pltpu.VMEM((tn, k), BF16), pltpu.VMEM((tn, k), BF16),
                        pltpu.VMEM((CARRY + tm, tn), F32)],
        compiler_params=_params(("arbitrary", "arbitrary")),
        name="ssd_xz_proj",
    )(a, w_in_t, w_in_t, conv_w, conv_b.reshape(1, SSD_CONV_DIM), vmask)


def _bc_kernel(a_ref, w_ref, cw_ref, cb_ref, o_ref, wb_ref, buf_ref, *, tiles_per_batch, width):
    _cast_weight(w_ref, wb_ref)
    _conv_carry_reset(buf_ref, tiles_per_batch)
    rows = a_ref.shape[0]
    a = a_ref[...]
    for lo in range(0, o_ref.shape[1], 2 * MXU_COLS):
        cols = slice(lo, lo + 2 * MXU_COLS)
        buf_ref[CARRY:CARRY + rows, cols] = _dot_nt(a, wb_ref[cols, :])
        o_ref[:, cols] = _silu(_causal_conv(buf_ref, cw_ref, cb_ref, rows, width, cols))
    _conv_carry_save(buf_ref, rows)


def _bc_proj(a, w_in_t, layer, conv_w, conv_b, length):
    rows, k = a.shape
    width = conv_w.shape[0]
    tm, tn = PROJ_ROW_TILE, WIDE_COL_TILE
    assert length % tm == 0
    conv_off = SSD_DI // tn
    w_off = 2 * SSD_DI // tn
    return pl.pallas_call(
        functools.partial(_bc_kernel, tiles_per_batch=length // tm, width=width),
        grid=(2 * SSD_BC // tn, rows // tm),
        in_specs=[pl.BlockSpec((tm, k), lambda j, i: (i, 0)),
                  pl.BlockSpec((None, tn, k), lambda j, i: (layer, j + w_off, 0)),
                  pl.BlockSpec((width, tn), lambda j, i: (0, j + conv_off)),
                  pl.BlockSpec((1, tn), lambda j, i: (0, j + conv_off))],
        out_specs=pl.BlockSpec((tm, tn), lambda j, i: (i, j)),
        out_shape=jax.ShapeDtypeStruct((rows, 2 * SSD_BC), F32),
        scratch_shapes=[pltpu.VMEM((tn, k), BF16), pltpu.VMEM((CARRY + tm, tn), F32)],
        compiler_params=_params(("arbitrary", "arbitrary")),
        name="ssd_bc_proj",
    )(a, w_in_t, conv_w, conv_b.reshape(1, SSD_CONV_DIM))


def _stack3(x, place_ref):
    parts = _split3(x)
    out = _dot(parts[0], place_ref[0]) + _dot(parts[1], place_ref[1]) + _dot(parts[2], place_ref[2])
    return out.astype(BF16)


def _dt_kernel(a_ref, w_ref, bias_ref, alog_ref, tri_ref, place_ref, dt_ref, cum_ref):
    dt = _softplus(_dot_nt(a_ref[...], w_ref[...].astype(BF16)) + bias_ref[...])
    da = dt * (-jnp.exp(alog_ref[...]))
    cum = _dot01(tri_ref[...], da, sel_first=True)
    dt_ref[...] = _stack3(dt, place_ref)
    cum_ref[...] = _stack3(cum, place_ref)


def _dt_path(a, w_in_t, layer, dt_bias, a_log):
    rows, k = a.shape
    dt_row_blk = (SSD_DI + SSD_CONV_DIM) // SSD_HEADS
    assert dt_row_blk * SSD_HEADS == SSD_DI + SSD_CONV_DIM
    r = np.arange(ROW_TILE)
    tri = ((r[:, None] // CHUNK == r[None, :] // CHUNK) & (r[:, None] >= r[None, :]))
    tri = jnp.asarray(tri, BF16)
    place = np.zeros((3, SSD_HEADS, SPLIT_LANES), np.float32)
    for term in range(3):
        place[term, np.arange(SSD_HEADS), term * SSD_HEADS + np.arange(SSD_HEADS)] = 1.0
    place = jnp.asarray(place, BF16)
    shp = jax.ShapeDtypeStruct((rows, SPLIT_LANES), BF16)
    return pl.pallas_call(
        _dt_kernel,
        grid=(rows // ROW_TILE,),
        in_specs=[pl.BlockSpec((ROW_TILE, k), lambda i: (i, 0)),
                  pl.BlockSpec((None, SSD_HEADS, k), lambda i: (layer, dt_row_blk, 0)),
                  pl.BlockSpec((1, SSD_HEADS), lambda i: (0, 0)),
                  pl.BlockSpec((1, SSD_HEADS), lambda i: (0, 0)),
                  pl.BlockSpec((ROW_TILE, ROW_TILE), lambda i: (0, 0)),
                  pl.BlockSpec((3, SSD_HEADS, SPLIT_LANES), lambda i: (0, 0, 0))],
        out_specs=[pl.BlockSpec((ROW_TILE, SPLIT_LANES), lambda i: (i, 0)),
                   pl.BlockSpec((ROW_TILE, SPLIT_LANES), lambda i: (i, 0))],
        out_shape=[shp, shp],
        compiler_params=_params(("arbitrary",)),
        name="ssd_dt_path",
    )(a, w_in_t, dt_bias.reshape(1, SSD_HEADS), a_log.reshape(1, SSD_HEADS), tri, place)


def _out_kernel(a_ref, w_ref, h_ref, m_ref, nw_ref, *refs, n_load, keep_residual):
    if keep_residual:
        ho_ref, hn_ref, wb_ref = refs
    else:
        hn_ref, wb_ref = refs
    step = pl.program_id(0)

    @pl.when(step < n_load)
    def _():
        r0 = pl.multiple_of(step * W_CHUNK, W_CHUNK)
        wb_ref[pl.ds(r0, W_CHUNK), :] = w_ref[...].astype(BF16)

    @pl.when(step >= n_load)
    def _():
        hnew = (h_ref[...] + _dot(a_ref[...], wb_ref[...])) * m_ref[...]
        if keep_residual:
            ho_ref[...] = hnew
        hn_ref[...] = _rms_scale(hnew, nw_ref[...]).astype(hn_ref.dtype)


def _out_row_tile(k):
    return 416 if k <= 4096 else 320


def _out_proj(a, w, layer, h, vmask, norm_w, hn_dtype, name, keep_residual=True):
    rows, k = a.shape
    d = w.shape[2]
    tm = _out_row_tile(k)
    n_load = k // W_CHUNK
    assert n_load * W_CHUNK == k and rows % tm == 0

    def row(step):
        return jnp.maximum(step - n_load, 0)

    row_blk = pl.BlockSpec((tm, d), lambda s: (row(s), 0))
    hn_shape = jax.ShapeDtypeStruct((rows, d), hn_dtype)
    if keep_residual:
        out_specs, out_shape = [row_blk, row_blk], [jax.ShapeDtypeStruct((rows, d), F32), hn_shape]
    else:
        out_specs, out_shape = [row_blk], [hn_shape]
    return pl.pallas_call(
        functools.partial(_out_kernel, n_load=n_load, keep_residual=keep_residual),
        grid=(n_load + rows // tm,),
        in_specs=[pl.BlockSpec((tm, k), lambda s: (row(s), 0)),
                  pl.BlockSpec((None, W_CHUNK, d), lambda s: (layer, jnp.minimum(s, n_load - 1), 0)),
                  row_blk,
                  pl.BlockSpec((tm, 1), lambda s: (row(s), 0)),
                  pl.BlockSpec((1, d), lambda s: (0, 0))],
        out_specs=out_specs,
        out_shape=out_shape,
        scratch_shapes=[pltpu.VMEM((k, d), BF16)],
        compiler_params=_params(("arbitrary",)),
        name=name,
    )(a, w, h, vmask, norm_w.reshape(1, d))


def _ret_scan_kernel(q_ref, k_ref, v_ref, gs_ref, gnw_ref, intra_ref, qd_ref, kd_ref, cd_ref,
                     y_ref, state_ref, *, chunks):
    @pl.when(pl.program_id(2) == 0)
    def _():
        state_ref[...] = jnp.zeros(state_ref.shape, F32)

    intra = intra_ref[0]
    q_decay = qd_ref[0]
    k_decay = kd_ref[0]
    chunk_decay = cd_ref[0]
    gnw = gnw_ref[...]

    def body(c, carry):
        r0 = c * RET_CHUNK
        q = q_ref[pl.ds(r0, RET_CHUNK), :]
        k = k_ref[pl.ds(r0, RET_CHUNK), :]
        v = v_ref[pl.ds(r0, RET_CHUNK), :]
        qb = q.astype(BF16)
        scores = _dot_nt(qb, k.astype(BF16)) * intra
        o = _dot(scores.astype(BF16), v)
        state = state_ref[...]
        o = o + q_decay * _dot(qb, state.astype(BF16))
        kd = (k * k_decay).astype(BF16)
        state_ref[...] = state * chunk_decay + _dot_tn(kd, v)
        o = o * lax.rsqrt(jnp.mean(o * o, axis=-1, keepdims=True) + NORM_EPS) * gnw
        y_ref[pl.ds(r0, RET_CHUNK), :] = (gs_ref[pl.ds(r0, RET_CHUNK), :] * o).astype(y_ref.dtype)
        return carry

    for c in range(chunks):
        body(c, 0)


def _ret_scan(qk, v, gs, gn_w, bsz, tiles_per_batch):
    rows = qk.shape[0]
    log_gamma = np.log1p(-np.exp2(-5.0 - np.arange(RET_HEADS, dtype=np.float64)))
    chunk = RET_CHUNK
    assert ROW_TILE % chunk == 0
    idx = np.arange(chunk, dtype=np.float64)
    diff = idx[:, None] - idx[None, :]
    intra = np.where(diff[None] >= 0, np.exp(np.maximum(diff, 0.0)[None] * log_gamma[:, None, None]), 0.0)
    q_decay = np.exp((idx + 1.0)[None, :] * log_gamma[:, None])[..., None]
    k_decay = np.exp((chunk - 1.0 - idx)[None, :] * log_gamma[:, None])[..., None]
    chunk_decay = np.exp(chunk * log_gamma)[:, None, None]
    intra, q_decay, k_decay, chunk_decay = (jnp.asarray(t, F32) for t in (intra, q_decay, k_decay, chunk_decay))
    tpb = tiles_per_batch
    nkh = RET_QK // RET_DK
    return pl.pallas_call(
        functools.partial(_ret_scan_kernel, chunks=ROW_TILE // chunk),
        grid=(bsz, RET_HEADS, tpb),
        in_specs=[pl.BlockSpec((ROW_TILE, RET_DK), lambda b, h, t: (b * tpb + t, h)),
                  pl.BlockSpec((ROW_TILE, RET_DK), lambda b, h, t: (b * tpb + t, nkh + h)),
                  pl.BlockSpec((ROW_TILE, RET_DV), lambda b, h, t: (b * tpb + t, h)),
                  pl.BlockSpec((ROW_TILE, RET_DV), lambda b, h, t: (b * tpb + t, h)),
                  pl.BlockSpec((1, RET_DV), lambda b, h, t: (0, h)),
                  pl.BlockSpec((1, chunk, chunk), lambda b, h, t: (h, 0, 0)),
                  pl.BlockSpec((1, chunk, 1), lambda b, h, t: (h, 0, 0)),
                  pl.BlockSpec((1, chunk, 1), lambda b, h, t: (h, 0, 0)),
                  pl.BlockSpec((1, 1, 1), lambda b, h, t: (h, 0, 0))],
        out_specs=pl.BlockSpec((ROW_TILE, RET_DV), lambda b, h, t: (b * tpb + t, h)),
        out_shape=jax.ShapeDtypeStruct((rows, RET_VDIM), BF16),
        scratch_shapes=[pltpu.VMEM((RET_DK, RET_DV), F32)],
        compiler_params=_params(("arbitrary", "arbitrary", "arbitrary")),
        name="ret_scan",
    )(qk, qk, v, gs, gn_w.reshape(1, RET_VDIM), intra, q_decay, k_decay, chunk_decay)


def _ssd_scan_kernel(x_ref, b_ref, c_ref, z_ref, dt_ref, cum_ref, sel_ref, d_ref, gw_ref,
                     y_ref, state_ref, dte_ref, cume_ref, *, chunks):
    @pl.when(pl.program_id(2) == 0)
    def _():
        state_ref[...] = jnp.zeros(state_ref.shape, F32)

    sel = sel_ref[...]
    dte_ref[...] = _dot(dt_ref[...], sel)
    cume_ref[...] = _dot(cum_ref[...], sel)

    gwid = SSD_GW
    row = lax.broadcasted_iota(jnp.int32, (CHUNK, gwid), 0)
    lane_pos = lax.broadcasted_iota(jnp.int32, (CHUNK, gwid), 1) & (CHUNK - 1)
    causal = row >= lane_pos
    diag = row == lane_pos
    quad = 4 * SSD_HEADDIM
    blk_r = lax.broadcasted_iota(jnp.int32, (quad, quad), 0) // SSD_HEADDIM
    blk_c = lax.broadcasted_iota(jnp.int32, (quad, quad), 1) // SSD_HEADDIM
    blockdiag = blk_r == blk_c
    d_skip = d_ref[...]
    gw = gw_ref[...]

    def body(c, carry):
        r0 = c * CHUNK
        rows = pl.ds(r0, CHUNK)
        x = x_ref[rows, :]
        cum = cume_ref[rows, :]
        bb = b_ref[rows, :].astype(BF16)
        cb = c_ref[rows, :].astype(BF16)
        xdt = x * dte_ref[rows, :]
        cum_row = jnp.sum(jnp.where(diag, cum, 0.0), axis=0, keepdims=True)
        decay = jnp.where(causal, jnp.exp(cum - cum_row), 0.0)
        gram = _dot_nt(cb, jnp.concatenate([bb] * SSD_HPG, axis=0))
        attn = (gram * decay).astype(BF16)
        xdt_b = xdt.astype(BF16)
        ys = []
        for qd in range(gwid // quad):
            xq = xdt_b[:, qd * quad:(qd + 1) * quad]
            rhs = jnp.where(blockdiag, jnp.concatenate([xq] * 4, axis=0), jnp.zeros((), BF16))
            ys.append(_dot(attn[:, qd * quad:(qd + 1) * quad], rhs))
        y = jnp.concatenate(ys, axis=1)
        state = state_ref[...]
        y = y + _dot(cb, state.astype(BF16)) * jnp.exp(cum)
        cum_last = cum[CHUNK - 1:CHUNK, :]
        xw = (xdt * jnp.exp(cum_last - cum)).astype(BF16)
        state_ref[...] = state * jnp.exp(cum_last) + _dot_tn(bb, xw)
        y = (y + x * d_skip) * z_ref[rows, :]
        y = y * lax.rsqrt(jnp.mean(y * y, axis=-1, keepdims=True) + NORM_EPS) * gw
        y_ref[rows, :] = y.astype(y_ref.dtype)
        return carry

    for c in range(chunks):
        body(c, 0)


def _ssd_scan(xs, bc, zs, dt, cum, d_skip, gnorm_w, bsz, tiles_per_batch):
    rows = xs.shape[0]
    tpb = tiles_per_batch
    b_blk = 0
    c_blk = SSD_BC // SSD_STATE
    d_exp = jnp.repeat(d_skip, SSD_HEADDIM).reshape(1, SSD_DI)
    one_hot = np.arange(SSD_HEADS)[:, None] == (np.arange(SSD_DI)[None, :] // SSD_HEADDIM)
    expand = np.zeros((SPLIT_LANES, SSD_DI), np.float32)
    expand[:3 * SSD_HEADS] = np.tile(one_hot, (3, 1))
    expand = jnp.asarray(expand, BF16)
    wide = pl.BlockSpec((ROW_TILE, SSD_GW), lambda b, g, t: (b * tpb + t, g))
    heads = pl.BlockSpec((ROW_TILE, SPLIT_LANES), lambda b, g, t: (b * tpb + t, 0))
    return pl.pallas_call(
        functools.partial(_ssd_scan_kernel, chunks=ROW_TILE // CHUNK),
        grid=(bsz, SSD_GROUPS, tpb),
        in_specs=[wide,
                  pl.BlockSpec((ROW_TILE, SSD_STATE), lambda b, g, t: (b * tpb + t, b_blk + g)),
                  pl.BlockSpec((ROW_TILE, SSD_STATE), lambda b, g, t: (b * tpb + t, c_blk + g)),
                  wide, heads, heads,
                  pl.BlockSpec((SPLIT_LANES, SSD_GW), lambda b, g, t: (0, g)),
                  pl.BlockSpec((1, SSD_GW), lambda b, g, t: (0, g)),
                  pl.BlockSpec((1, SSD_GW), lambda b, g, t: (0, g))],
        out_specs=wide,
        out_shape=jax.ShapeDtypeStruct((rows, SSD_DI), BF16),
        scratch_shapes=[pltpu.VMEM((SSD_STATE, SSD_GW), F32),
                        pltpu.VMEM((ROW_TILE, SSD_GW), F32), pltpu.VMEM((ROW_TILE, SSD_GW), F32)],
        compiler_params=_params(("arbitrary", "arbitrary", "arbitrary")),
        name="ssd_scan",
    )(xs, bc, bc, zs, dt, cum, expand, d_exp, gnorm_w.reshape(1, SSD_DI))


def kernel(x, meta_tokens, ret_norm_w, ret_w_in, ret_gn_w, ret_w_out, ssd_norm_w, ssd_w_in, ssd_conv_w, ssd_conv_b, ssd_dt_bias, ssd_a_log, ssd_d, ssd_gnorm_w, ssd_w_out, ffn_norm_w, ffn_w_up, ffn_conv_w, ffn_conv_b, ffn_w_down, final_norm_w):
    bsz, seq, d = x.shape
    length = META_PAD + N_META + seq
    assert d == D_MODEL and length % ROW_TILE == 0 and ROW_TILE % CHUNK == 0 and length % PROJ_ROW_TILE == 0
    tpb = length // ROW_TILE
    depth = ffn_w_up.shape[0]

    meta = jnp.broadcast_to(meta_tokens.astype(x.dtype)[None], (bsz, N_META, d))
    h = jnp.concatenate([jnp.zeros((bsz, META_PAD, d), x.dtype), meta, x], axis=1).reshape(bsz * length, d)

    pos_i = np.arange(length) - META_PAD
    vmask = jnp.asarray(np.tile((pos_i >= 0).astype(np.float32), bsz).reshape(bsz * length, 1))
    half = RET_DK // 2
    inv = ROPE_BASE ** (-jnp.arange(half, dtype=F32) / half)
    ang = jnp.asarray(pos_i, F32)[:, None] * inv[None, :]
    cos, sin = jnp.cos(ang), jnp.sin(ang)

    ssd_w_in_t = jnp.swapaxes(ssd_w_in, 1, 2)
    hn = _rmsnorm(h, ret_norm_w[0])
    for i in range(depth):
        j = i // 2
        if i % 2 == 0:
            qk = _qk_proj(hn, ret_w_in, j, cos, sin, length // PROJ_ROW_TILE)
            v = _proj(hn, ret_w_in, j, 2 * RET_QK, RET_VDIM, BF16, name="ret_v_proj")
            gs = _proj(hn, ret_w_in, j, 2 * RET_QK + RET_VDIM, RET_VDIM, F32, act="silu", name="ret_g_proj")
            y = _ret_scan(qk, v, gs, ret_gn_w[j], bsz, tpb)
            h, hn = _out_proj(y, ret_w_out, j, h, vmask, ffn_norm_w[i], BF16, name="ret_out_proj")
        else:
            xs, zs = _xz_proj(hn, ssd_w_in_t, j, ssd_conv_w[j], ssd_conv_b[j], vmask, length)
            bc = _bc_proj(hn, ssd_w_in_t, j, ssd_conv_w[j], ssd_conv_b[j], length)
            dt, cum = _dt_path(hn, ssd_w_in_t, j, ssd_dt_bias[j], ssd_a_log[j])
            y = _ssd_scan(xs, bc, zs, dt, cum, ssd_d[j], ssd_gnorm_w[j], bsz, tpb)
            h, hn = _out_proj(y, ssd_w_out, j, h, vmask, ffn_norm_w[i], BF16, name="ssd_out_proj")
        act = _ffn_up(hn, ffn_w_up, i, ffn_conv_w[i], ffn_conv_b[i], length)
        if i == depth - 1:
            (hn,) = _out_proj(act, ffn_w_down, i, h, vmask, final_norm_w, x.dtype,
                              name="ffn_down_final", keep_residual=False)
        else:
            next_w = ssd_norm_w[(i + 1) // 2] if (i + 1) % 2 == 1 else ret_norm_w[(i + 1) // 2]
            h, hn = _out_proj(act, ffn_w_down, i, h, vmask, next_w, BF16, name="ffn_down_proj")
    return hn.reshape(bsz, length, d)[:, META_PAD + N_META:]
```

```python
import functools

import jax
import jax.numpy as jnp
import numpy as np
from jax import lax
from jax.experimental import pallas as pl
from jax.experimental.pallas import tpu as pltpu

F32 = jnp.float32
BF16 = jnp.bfloat16

D_MODEL = 2048
CHUNK = 64
N_META = 16
META_PAD = CHUNK - N_META
NORM_EPS = 1e-6

RET_HEADS = 8
RET_CHUNK = 208
RET_DK = 256
RET_DV = 512
RET_QK = RET_HEADS * RET_DK
RET_VDIM = RET_HEADS * RET_DV
ROPE_BASE = 10000.0

SSD_DI = 4096
SSD_HEADDIM = 64
SSD_HEADS = 64
SSD_GROUPS = 8
SSD_HPG = 8
SSD_STATE = 128
SSD_BC = SSD_GROUPS * SSD_STATE
SSD_CONV_DIM = SSD_DI + 2 * SSD_BC
SSD_GW = SSD_HPG * SSD_HEADDIM
SPLIT_LANES = 256

FFN_DIM = 5632

V7X_VMEM_BYTES = 64 * 1024 * 1024
VMEM_LIMIT = V7X_VMEM_BYTES - 8 * 1024 * 1024

ROW_TILE = 832
PROJ_ROW_TILE = 1040
COL_TILE = 512
WIDE_COL_TILE = 1024
W_CHUNK = 256
CARRY = 8
MXU_COLS = 256


def _params(sem):
    return pltpu.CompilerParams(dimension_semantics=sem, vmem_limit_bytes=VMEM_LIMIT)


def _sigmoid(x):
    return 1.0 / (1.0 + jnp.exp(-x))


def _silu(x):
    return x * _sigmoid(x)


def _softplus(x):
    return jnp.maximum(x, 0.0) + jnp.log1p(jnp.exp(-jnp.abs(x)))


def _dot(a, b):
    return jnp.dot(a, b, preferred_element_type=F32)


def _dot_nt(a, b):
    return lax.dot_general(a, b, (((1,), (1,)), ((), ())), preferred_element_type=F32)


def _dot_tn(a, b):
    return lax.dot_general(a, b, (((0,), (0,)), ((), ())), preferred_element_type=F32)


def _split3(x):
    hi = x.astype(BF16)
    r1 = x - hi.astype(F32)
    mid = r1.astype(BF16)
    lo = (r1 - mid.astype(F32)).astype(BF16)
    return hi, mid, lo


def _dot01(sel, x, sel_first):
    parts = _split3(x)
    if sel_first:
        return _dot(sel, parts[0]) + _dot(sel, parts[1]) + _dot(sel, parts[2])
    return _dot(parts[0], sel) + _dot(parts[1], sel) + _dot(parts[2], sel)


def _rms_scale(x, w):
    ms = jnp.mean(x * x, axis=-1, keepdims=True)
    return x * lax.rsqrt(ms + NORM_EPS) * w


def _cast_weight(w_ref, wb_ref):
    @pl.when(pl.program_id(1) == 0)
    def _():
        wb_ref[...] = w_ref[...].astype(BF16)


def _rmsnorm_kernel(x_ref, w_ref, o_ref):
    o_ref[...] = _rms_scale(x_ref[...], w_ref[...]).astype(o_ref.dtype)


def _rmsnorm(x, w):
    rows, d = x.shape
    return pl.pallas_call(
        _rmsnorm_kernel,
        grid=(rows // ROW_TILE,),
        in_specs=[pl.BlockSpec((ROW_TILE, d), lambda i: (i, 0)),
                  pl.BlockSpec((1, d), lambda i: (0, 0))],
        out_specs=pl.BlockSpec((ROW_TILE, d), lambda i: (i, 0)),
        out_shape=jax.ShapeDtypeStruct((rows, d), BF16),
        compiler_params=_params(("arbitrary",)),
        name="rmsnorm",
    )(x, w.reshape(1, d))


def _proj_kernel(a_ref, w_ref, o_ref, wb_ref, *, act):
    _cast_weight(w_ref, wb_ref)
    acc = _dot(a_ref[...], wb_ref[...])
    if act == "silu":
        acc = _silu(acc)
    o_ref[...] = acc.astype(o_ref.dtype)


def _proj(a, w, layer, col0, ncols, out_dtype, act=None, name="proj"):
    rows, k = a.shape
    off = col0 // WIDE_COL_TILE
    return pl.pallas_call(
        functools.partial(_proj_kernel, act=act),
        grid=(ncols // WIDE_COL_TILE, rows // PROJ_ROW_TILE),
        in_specs=[pl.BlockSpec((PROJ_ROW_TILE, k), lambda j, i: (i, 0)),
                  pl.BlockSpec((None, k, WIDE_COL_TILE), lambda j, i: (layer, 0, j + off))],
        out_specs=pl.BlockSpec((PROJ_ROW_TILE, WIDE_COL_TILE), lambda j, i: (i, j)),
        out_shape=jax.ShapeDtypeStruct((rows, ncols), out_dtype),
        scratch_shapes=[pltpu.VMEM((k, WIDE_COL_TILE), BF16)],
        compiler_params=_params(("arbitrary", "arbitrary")),
        name=name,
    )(a, w)


def _qk_kernel(a_ref, w_ref, cos_ref, sin_ref, o_ref, wb_ref, *, nq_tiles):
    _cast_weight(w_ref, wb_ref)
    acc = _dot(a_ref[...], wb_ref[...])
    scale = jnp.where(pl.program_id(0) >= nq_tiles, RET_DK ** -0.5, 1.0).astype(F32)
    cos = cos_ref[...] * scale
    sin = sin_ref[...] * scale
    half = RET_DK // 2
    for hd in range(WIDE_COL_TILE // RET_DK):
        lo = hd * RET_DK
        x1 = acc[:, lo:lo + half]
        x2 = acc[:, lo + half:lo + RET_DK]
        o_ref[:, lo:lo + half] = x1 * cos - x2 * sin
        o_ref[:, lo + half:lo + RET_DK] = x1 * sin + x2 * cos


def _qk_proj(a, w, layer, cos, sin, tiles_per_batch):
    rows, k = a.shape
    half = RET_DK // 2
    return pl.pallas_call(
        functools.partial(_qk_kernel, nq_tiles=RET_QK // WIDE_COL_TILE),
        grid=(2 * RET_QK // WIDE_COL_TILE, rows // PROJ_ROW_TILE),
        in_specs=[pl.BlockSpec((PROJ_ROW_TILE, k), lambda j, i: (i, 0)),
                  pl.BlockSpec((None, k, WIDE_COL_TILE), lambda j, i: (layer, 0, j)),
                  pl.BlockSpec((PROJ_ROW_TILE, half), lambda j, i: (i % tiles_per_batch, 0)),
                  pl.BlockSpec((PROJ_ROW_TILE, half), lambda j, i: (i % tiles_per_batch, 0))],
        out_specs=pl.BlockSpec((PROJ_ROW_TILE, WIDE_COL_TILE), lambda j, i: (i, j)),
        out_shape=jax.ShapeDtypeStruct((rows, 2 * RET_QK), F32),
        scratch_shapes=[pltpu.VMEM((k, WIDE_COL_TILE), BF16)],
        compiler_params=_params(("arbitrary", "arbitrary")),
        name="ret_qk_proj",
    )(a, w, cos, sin)


def _causal_conv(buf_ref, cw_ref, cb_ref, rows, width, cols=slice(None)):
    cw = cw_ref[:, cols]
    out = cb_ref[:, cols] + cw[width - 1:width, :] * buf_ref[CARRY:CARRY + rows, cols]
    for tap in range(width - 1):
        shift = width - 1 - tap
        out = out + cw[tap:tap + 1, :] * buf_ref[CARRY - shift:CARRY - shift + rows, cols]
    return out


def _conv_carry_reset(buf_ref, tiles_per_batch):
    @pl.when(pl.program_id(1) % tiles_per_batch == 0)
    def _():
        buf_ref[0:CARRY, :] = jnp.zeros((CARRY, buf_ref.shape[1]), F32)


def _conv_carry_save(buf_ref, rows):
    buf_ref[0:CARRY, :] = buf_ref[rows:rows + CARRY, :]


def _ffn_up_kernel(a_ref, wg_ref, wu_ref, cw_ref, cb_ref, o_ref, wgb_ref, wub_ref, buf_ref,
                   *, tiles_per_batch, width):
    _cast_weight(wg_ref, wgb_ref)
    _cast_weight(wu_ref, wub_ref)
    _conv_carry_reset(buf_ref, tiles_per_batch)
    rows = a_ref.shape[0]
    a = a_ref[...]
    buf_ref[CARRY:CARRY + rows, :] = _dot(a, wgb_ref[...])
    up = _dot(a, wub_ref[...])
    gate = _causal_conv(buf_ref, cw_ref, cb_ref, rows, width)
    o_ref[...] = (_silu(gate) * up).astype(o_ref.dtype)
    _conv_carry_save(buf_ref, rows)


def _ffn_up(a, w_up, layer, conv_w, conv_b, length):
    rows, k = a.shape
    width = conv_w.shape[0]
    nt = FFN_DIM // COL_TILE
    tm = PROJ_ROW_TILE
    assert length % tm == 0
    return pl.pallas_call(
        functools.partial(_ffn_up_kernel, tiles_per_batch=length // tm, width=width),
        grid=(nt, rows // tm),
        in_specs=[pl.BlockSpec((tm, k), lambda j, i: (i, 0)),
                  pl.BlockSpec((None, k, COL_TILE), lambda j, i: (layer, 0, j)),
                  pl.BlockSpec((None, k, COL_TILE), lambda j, i: (layer, 0, j + nt)),
                  pl.BlockSpec((width, COL_TILE), lambda j, i: (0, j)),
                  pl.BlockSpec((1, COL_TILE), lambda j, i: (0, j))],
        out_specs=pl.BlockSpec((tm, COL_TILE), lambda j, i: (i, j)),
        out_shape=jax.ShapeDtypeStruct((rows, FFN_DIM), BF16),
        scratch_shapes=[pltpu.VMEM((k, COL_TILE), BF16), pltpu.VMEM((k, COL_TILE), BF16),
                        pltpu.VMEM((CARRY + tm, COL_TILE), F32)],
        compiler_params=_params(("arbitrary", "arbitrary")),
        name="ffn_up",
    )(a, w_up, w_up, conv_w, conv_b.reshape(1, FFN_DIM))


def _xz_kernel(a_ref, wx_ref, wz_ref, cw_ref, cb_ref, m_ref, xo_ref, zo_ref, wxb_ref, wzb_ref, buf_ref,
               *, tiles_per_batch, width):
    _cast_weight(wx_ref, wxb_ref)
    _cast_weight(wz_ref, wzb_ref)
    _conv_carry_reset(buf_ref, tiles_per_batch)
    rows = a_ref.shape[0]
    a = a_ref[...]
    buf_ref[CARRY:CARRY + rows, :] = _dot_nt(a, wxb_ref[...])
    z = _dot_nt(a, wzb_ref[...])
    xo_ref[...] = _silu(_causal_conv(buf_ref, cw_ref, cb_ref, rows, width)) * m_ref[...]
    zo_ref[...] = _silu(z).astype(zo_ref.dtype)
    _conv_carry_save(buf_ref, rows)


def _xz_proj(a, w_in_t, layer, conv_w, conv_b, vmask, length):
    rows, k = a.shape
    width = conv_w.shape[0]
    tm, tn = PROJ_ROW_TILE, COL_TILE
    assert length % tm == 0
    x_off = SSD_DI // tn
    w_blk = lambda off: pl.BlockSpec((None, tn, k), lambda j, i: (layer, j + off, 0))
    out_blk = pl.BlockSpec((tm, tn), lambda j, i: (i, j))
    shapes = [jax.ShapeDtypeStruct((rows, SSD_DI), F32), jax.ShapeDtypeStruct((rows, SSD_DI), BF16)]
    return pl.pallas_call(
        functools.partial(_xz_kernel, tiles_per_batch=length // tm, width=width),
        grid=(SSD_DI // tn, rows // tm),
        in_specs=[pl.BlockSpec((tm, k), lambda j, i: (i, 0)),
                  w_blk(x_off), w_blk(0),
                  pl.BlockSpec((width, tn), lambda j, i: (0, j)),
                  pl.BlockSpec((1, tn), lambda j, i: (0, j)),
                  pl.BlockSpec((tm, 1), lambda j, i: (i, 0))],
        out_specs=[out_blk, out_blk],
        out_shape=shapes,
        scratch_shapes=[pltpu.VMEM((tn, k), BF16), pltpu.VMEM((tn, k), BF16),
                        pltpu.VMEM((CARRY + tm, tn), F32)],
        compiler_params=_params(("arbitrary", "arbitrary")),
        name="ssd_xz_proj",
    )(a, w_in_t, w_in_t, conv_w, conv_b.reshape(1, SSD_CONV_DIM), vmask)


def _bc_kernel(a_ref, w_ref, cw_ref, cb_ref, o_ref, wb_ref, buf_ref, *, tiles_per_batch, width):
    _cast_weight(w_ref, wb_ref)
    _conv_carry_reset(buf_ref, tiles_per_batch)
    rows = a_ref.shape[0]
    a = a_ref[...]
    for lo in range(0, o_ref.shape[1], 2 * MXU_COLS):
        cols = slice(lo, lo + 2 * MXU_COLS)
        buf_ref[CARRY:CARRY + rows, cols] = _dot_nt(a, wb_ref[cols, :])
        o_ref[:, cols] = _silu(_causal_conv(buf_ref, cw_ref, cb_ref, rows, width, cols)).astype(o_ref.dtype)
    _conv_carry_save(buf_ref, rows)


def _bc_proj(a, w_in_t, layer, conv_w, conv_b, length):
    rows, k = a.shape
    width = conv_w.shape[0]
    tm, tn = PROJ_ROW_TILE, WIDE_COL_TILE
    assert length % tm == 0
    conv_off = SSD_DI // tn
    w_off = 2 * SSD_DI // tn
    return pl.pallas_call(
        functools.partial(_bc_kernel, tiles_per_batch=length // tm, width=width),
        grid=(2 * SSD_BC // tn, rows // tm),
        in_specs=[pl.BlockSpec((tm, k), lambda j, i: (i, 0)),
                  pl.BlockSpec((None, tn, k), lambda j, i: (layer, j + w_off, 0)),
                  pl.BlockSpec((width, tn), lambda j, i: (0, j + conv_off)),
                  pl.BlockSpec((1, tn), lambda j, i: (0, j + conv_off))],
        out_specs=pl.BlockSpec((tm, tn), lambda j, i: (i, j)),
        out_shape=jax.ShapeDtypeStruct((rows, 2 * SSD_BC), BF16),
        scratch_shapes=[pltpu.VMEM((tn, k), BF16), pltpu.VMEM((CARRY + tm, tn), F32)],
        compiler_params=_params(("arbitrary", "arbitrary")),
        name="ssd_bc_proj",
    )(a, w_in_t, conv_w, conv_b.reshape(1, SSD_CONV_DIM))


def _stack3(x, place_ref):
    parts = _split3(x)
    out = _dot(parts[0], place_ref[0]) + _dot(parts[1], place_ref[1]) + _dot(parts[2], place_ref[2])
    return out.astype(BF16)


def _dt_kernel(a_ref, w_ref, bias_ref, alog_ref, tri_ref, place_ref, dt_ref, cum_ref):
    dt = _softplus(_dot_nt(a_ref[...], w_ref[...].astype(BF16)) + bias_ref[...])
    da = dt * (-jnp.exp(alog_ref[...]))
    cum = _dot01(tri_ref[...], da, sel_first=True)
    dt_ref[...] = _stack3(dt, place_ref)
    cum_ref[...] = _stack3(cum, place_ref)


def _dt_path(a, w_in_t, layer, dt_bias, a_log):
    rows, k = a.shape
    dt_row_blk = (SSD_DI + SSD_CONV_DIM) // SSD_HEADS
    assert dt_row_blk * SSD_HEADS == SSD_DI + SSD_CONV_DIM
    r = np.arange(ROW_TILE)
    tri = ((r[:, None] // CHUNK == r[None, :] // CHUNK) & (r[:, None] >= r[None, :]))
    tri = jnp.asarray(tri, BF16)
    place = np.zeros((3, SSD_HEADS, SPLIT_LANES), np.float32)
    for term in range(3):
        place[term, np.arange(SSD_HEADS), term * SSD_HEADS + np.arange(SSD_HEADS)] = 1.0
    place = jnp.asarray(place, BF16)
    shp = jax.ShapeDtypeStruct((rows, SPLIT_LANES), BF16)
    return pl.pallas_call(
        _dt_kernel,
        grid=(rows // ROW_TILE,),
        in_specs=[pl.BlockSpec((ROW_TILE, k), lambda i: (i, 0)),
                  pl.BlockSpec((None, SSD_HEADS, k), lambda i: (layer, dt_row_blk, 0)),
                  pl.BlockSpec((1, SSD_HEADS), lambda i: (0, 0)),
                  pl.BlockSpec((1, SSD_HEADS), lambda i: (0, 0)),
                  pl.BlockSpec((ROW_TILE, ROW_TILE), lambda i: (0, 0)),
                  pl.BlockSpec((3, SSD_HEADS, SPLIT_LANES), lambda i: (0, 0, 0))],
        out_specs=[pl.BlockSpec((ROW_TILE, SPLIT_LANES), lambda i: (i, 0)),
                   pl.BlockSpec((ROW_TILE, SPLIT_LANES), lambda i: (i, 0))],
        out_shape=[shp, shp],
        compiler_params=_params(("arbitrary",)),
        name="ssd_dt_path",
    )(a, w_in_t, dt_bias.reshape(1, SSD_HEADS), a_log.reshape(1, SSD_HEADS), tri, place)


def _out_kernel(a_ref, w_ref, h_ref, m_ref, nw_ref, *refs, n_load, keep_residual):
    if keep_residual:
        ho_ref, hn_ref, wb_ref = refs
    else:
        hn_ref, wb_ref = refs
    step = pl.program_id(0)

    @pl.when(step < n_load)
    def _():
        r0 = pl.multiple_of(step * W_CHUNK, W_CHUNK)
        wb_ref[pl.ds(r0, W_CHUNK), :] = w_ref[...].astype(BF16)

    @pl.when(step >= n_load)
    def _():
        hnew = (h_ref[...] + _dot(a_ref[...], wb_ref[...])) * m_ref[...]
        if keep_residual:
            ho_ref[...] = hnew
        hn_ref[...] = _rms_scale(hnew, nw_ref[...]).astype(hn_ref.dtype)


def _out_row_tile(k):
    return 416 if k <= 4096 else 320


def _out_proj(a, w, layer, h, vmask, norm_w, hn_dtype, name, keep_residual=True):
    rows, k = a.shape
    d = w.shape[2]
    tm = _out_row_tile(k)
    n_load = k // W_CHUNK
    assert n_load * W_CHUNK == k and rows % tm == 0

    def row(step):
        return jnp.maximum(step - n_load, 0)

    row_blk = pl.BlockSpec((tm, d), lambda s: (row(s), 0))
    hn_shape = jax.ShapeDtypeStruct((rows, d), hn_dtype)
    if keep_residual:
        out_specs, out_shape = [row_blk, row_blk], [jax.ShapeDtypeStruct((rows, d), F32), hn_shape]
    else:
        out_specs, out_shape = [row_blk], [hn_shape]
    return pl.pallas_call(
        functools.partial(_out_kernel, n_load=n_load, keep_residual=keep_residual),
        grid=(n_load + rows // tm,),
        in_specs=[pl.BlockSpec((tm, k), lambda s: (row(s), 0)),
                  pl.BlockSpec((None, W_CHUNK, d), lambda s: (layer, jnp.minimum(s, n_load - 1), 0)),
                  row_blk,
                  pl.BlockSpec((tm, 1), lambda s: (row(s), 0)),
                  pl.BlockSpec((1, d), lambda s: (0, 0))],
        out_specs=out_specs,
        out_shape=out_shape,
        scratch_shapes=[pltpu.VMEM((k, d), BF16)],
        compiler_params=_params(("arbitrary",)),
        name=name,
    )(a, w, h, vmask, norm_w.reshape(1, d))


def _ret_scan_kernel(q_ref, k_ref, v_ref, gs_ref, gnw_ref, intra_ref, qd_ref, kd_ref, cd_ref,
                     y_ref, state_ref, *, chunks):
    @pl.when(pl.program_id(2) == 0)
    def _():
        state_ref[...] = jnp.zeros(state_ref.shape, F32)

    intra = intra_ref[0]
    q_decay = qd_ref[0]
    k_decay = kd_ref[0]
    chunk_decay = cd_ref[0]
    gnw = gnw_ref[...]

    def body(c, carry):
        r0 = c * RET_CHUNK
        q = q_ref[pl.ds(r0, RET_CHUNK), :]
        k = k_ref[pl.ds(r0, RET_CHUNK), :]
        v = v_ref[pl.ds(r0, RET_CHUNK), :]
        qb = q.astype(BF16)
        scores = _dot_nt(qb, k.astype(BF16)) * intra
        o = _dot(scores.astype(BF16), v)
        state = state_ref[...]
        o = o + q_decay * _dot(qb, state.astype(BF16))
        kd = (k * k_decay).astype(BF16)
        state_ref[...] = state * chunk_decay + _dot_tn(kd, v)
        o = o * lax.rsqrt(jnp.mean(o * o, axis=-1, keepdims=True) + NORM_EPS) * gnw
        y_ref[pl.ds(r0, RET_CHUNK), :] = (gs_ref[pl.ds(r0, RET_CHUNK), :] * o).astype(y_ref.dtype)
        return carry

    for c in range(chunks):
        body(c, 0)


def _ret_scan(qk, v, gs, gn_w, bsz, tiles_per_batch):
    rows = qk.shape[0]
    log_gamma = np.log1p(-np.exp2(-5.0 - np.arange(RET_HEADS, dtype=np.float64)))
    chunk = RET_CHUNK
    assert ROW_TILE % chunk == 0
    idx = np.arange(chunk, dtype=np.float64)
    diff = idx[:, None] - idx[None, :]
    intra = np.where(diff[None] >= 0, np.exp(np.maximum(diff, 0.0)[None] * log_gamma[:, None, None]), 0.0)
    q_decay = np.exp((idx + 1.0)[None, :] * log_gamma[:, None])[..., None]
    k_decay = np.exp((chunk - 1.0 - idx)[None, :] * log_gamma[:, None])[..., None]
    chunk_decay = np.exp(chunk * log_gamma)[:, None, None]
    intra, q_decay, k_decay, chunk_decay = (jnp.asarray(t, F32) for t in (intra, q_decay, k_decay, chunk_decay))
    tpb = tiles_per_batch
    nkh = RET_QK // RET_DK
    return pl.pallas_call(
        functools.partial(_ret_scan_kernel, chunks=ROW_TILE // chunk),
        grid=(bsz, RET_HEADS, tpb),
        in_specs=[pl.BlockSpec((ROW_TILE, RET_DK), lambda b, h, t: (b * tpb + t, h)),
                  pl.BlockSpec((ROW_TILE, RET_DK), lambda b, h, t: (b * tpb + t, nkh + h)),
                  pl.BlockSpec((ROW_TILE, RET_DV), lambda b, h, t: (b * tpb + t, h)),
                  pl.BlockSpec((ROW_TILE, RET_DV), lambda b, h, t: (b * tpb + t, h)),
                  pl.BlockSpec((1, RET_DV), lambda b, h, t: (0, h)),
                  pl.BlockSpec((1, chunk, chunk), lambda b, h, t: (h, 0, 0)),
                  pl.BlockSpec((1, chunk, 1), lambda b, h, t: (h, 0, 0)),
                  pl.BlockSpec((1, chunk, 1), lambda b, h, t: (h, 0, 0)),
                  pl.BlockSpec((1, 1, 1), lambda b, h, t: (h, 0, 0))],
        out_specs=pl.BlockSpec((ROW_TILE, RET_DV), lambda b, h, t: (b * tpb + t, h)),
        out_shape=jax.ShapeDtypeStruct((rows, RET_VDIM), BF16),
        scratch_shapes=[pltpu.VMEM((RET_DK, RET_DV), F32)],
        compiler_params=_params(("arbitrary", "arbitrary", "arbitrary")),
        name="ret_scan",
    )(qk, qk, v, gs, gn_w.reshape(1, RET_VDIM), intra, q_decay, k_decay, chunk_decay)


def _ssd_scan_kernel(x_ref, b_ref, c_ref, z_ref, dt_ref, cum_ref, sel_ref, d_ref, gw_ref,
                     y_ref, state_ref, dte_ref, cume_ref, *, chunks):
    @pl.when(pl.program_id(2) == 0)
    def _():
        state_ref[...] = jnp.zeros(state_ref.shape, F32)

    sel = sel_ref[...]
    dte_ref[...] = _dot(dt_ref[...], sel)
    cume_ref[...] = _dot(cum_ref[...], sel)

    gwid = SSD_GW
    row = lax.broadcasted_iota(jnp.int32, (CHUNK, gwid), 0)
    lane_pos = lax.broadcasted_iota(jnp.int32, (CHUNK, gwid), 1) & (CHUNK - 1)
    causal = row >= lane_pos
    diag = row == lane_pos
    quad = 4 * SSD_HEADDIM
    blk_r = lax.broadcasted_iota(jnp.int32, (quad, quad), 0) // SSD_HEADDIM
    blk_c = lax.broadcasted_iota(jnp.int32, (quad, quad), 1) // SSD_HEADDIM
    blockdiag = blk_r == blk_c
    d_skip = d_ref[...]
    gw = gw_ref[...]

    def body(c, carry):
        r0 = c * CHUNK
        rows = pl.ds(r0, CHUNK)
        x = x_ref[rows, :]
        cum = cume_ref[rows, :]
        bb = b_ref[rows, :].astype(BF16)
        cb = c_ref[rows, :].astype(BF16)
        xdt = x * dte_ref[rows, :]
        cum_row = jnp.sum(jnp.where(diag, cum, 0.0), axis=0, keepdims=True)
        decay = jnp.where(causal, jnp.exp(cum - cum_row), 0.0)
        gram = _dot_nt(cb, jnp.concatenate([bb] * SSD_HPG, axis=0))
        attn = (gram * decay).astype(BF16)
        xdt_b = xdt.astype(BF16)
        ys = []
        for qd in range(gwid // quad):
            xq = xdt_b[:, qd * quad:(qd + 1) * quad]
            rhs = jnp.where(blockdiag, jnp.concatenate([xq] * 4, axis=0), jnp.zeros((), BF16))
            ys.append(_dot(attn[:, qd * quad:(qd + 1) * quad], rhs))
        y = jnp.concatenate(ys, axis=1)
        state = state_ref[...]
        y = y + _dot(cb, state.astype(BF16)) * jnp.exp(cum)
        cum_last = cum[CHUNK - 1:CHUNK, :]
        xw = (xdt * jnp.exp(cum_last - cum)).astype(BF16)
        state_ref[...] = state * jnp.exp(cum_last) + _dot_tn(bb, xw)
        y = (y + x * d_skip) * z_ref[rows, :]
        y = y * lax.rsqrt(jnp.mean(y * y, axis=-1, keepdims=True) + NORM_EPS) * gw
        y_ref[rows, :] = y.astype(y_ref.dtype)
        return carry

    for c in range(chunks):
        body(c, 0)


def _ssd_scan(xs, bc, zs, dt, cum, d_skip, gnorm_w, bsz, tiles_per_batch):
    rows = xs.shape[0]
    tpb = tiles_per_batch
    b_blk = 0
    c_blk = SSD_BC // SSD_STATE
    d_exp = jnp.repeat(d_skip, SSD_HEADDIM).reshape(1, SSD_DI)
    one_hot = np.arange(SSD_HEADS)[:, None] == (np.arange(SSD_DI)[None, :] // SSD_HEADDIM)
    expand = np.zeros((SPLIT_LANES, SSD_DI), np.float32)
    expand[:3 * SSD_HEADS] = np.tile(one_hot, (3, 1))
    expand = jnp.asarray(expand, BF16)
    wide = pl.BlockSpec((ROW_TILE, SSD_GW), lambda b, g, t: (b * tpb + t, g))
    heads = pl.BlockSpec((ROW_TILE, SPLIT_LANES), lambda b, g, t: (b * tpb + t, 0))
    return pl.pallas_call(
        functools.partial(_ssd_scan_kernel, chunks=ROW_TILE // CHUNK),
        grid=(bsz, SSD_GROUPS, tpb),
        in_specs=[wide,
                  pl.BlockSpec((ROW_TILE, SSD_STATE), lambda b, g, t: (b * tpb + t, b_blk + g)),
                  pl.BlockSpec((ROW_TILE, SSD_STATE), lambda b, g, t: (b * tpb + t, c_blk + g)),
                  wide, heads, heads,
                  pl.BlockSpec((SPLIT_LANES, SSD_GW), lambda b, g, t: (0, g)),
                  pl.BlockSpec((1, SSD_GW), lambda b, g, t: (0, g)),
                  pl.BlockSpec((1, SSD_GW), lambda b, g, t: (0, g))],
        out_specs=wide,
        out_shape=jax.ShapeDtypeStruct((rows, SSD_DI), BF16),
        scratch_shapes=[pltpu.VMEM((SSD_STATE, SSD_GW), F32),
                        pltpu.VMEM((ROW_TILE, SSD_GW), F32), pltpu.VMEM((ROW_TILE, SSD_GW), F32)],
        compiler_params=_params(("arbitrary", "arbitrary", "arbitrary")),
        name="ssd_scan",
    )(xs, bc, bc, zs, dt, cum, expand, d_exp, gnorm_w.reshape(1, SSD_DI))


def kernel(x, meta_tokens, ret_norm_w, ret_w_in, ret_gn_w, ret_w_out, ssd_norm_w, ssd_w_in, ssd_conv_w, ssd_conv_b, ssd_dt_bias, ssd_a_log, ssd_d, ssd_gnorm_w, ssd_w_out, ffn_norm_w, ffn_w_up, ffn_conv_w, ffn_conv_b, ffn_w_down, final_norm_w):
    bsz, seq, d = x.shape
    length = META_PAD + N_META + seq
    assert d == D_MODEL and length % ROW_TILE == 0 and ROW_TILE % CHUNK == 0 and length % PROJ_ROW_TILE == 0
    tpb = length // ROW_TILE
    depth = ffn_w_up.shape[0]

    meta = jnp.broadcast_to(meta_tokens.astype(x.dtype)[None], (bsz, N_META, d))
    h = jnp.concatenate([jnp.zeros((bsz, META_PAD, d), x.dtype), meta, x], axis=1).reshape(bsz * length, d)

    pos_i = np.arange(length) - META_PAD
    vmask = jnp.asarray(np.tile((pos_i >= 0).astype(np.float32), bsz).reshape(bsz * length, 1))
    half = RET_DK // 2
    inv = ROPE_BASE ** (-jnp.arange(half, dtype=F32) / half)
    ang = jnp.asarray(pos_i, F32)[:, None] * inv[None, :]
    cos, sin = jnp.cos(ang), jnp.sin(ang)

    ssd_w_in_t = jnp.swapaxes(ssd_w_in, 1, 2)
    hn = _rmsnorm(h, ret_norm_w[0])
    for i in range(depth):
        j = i // 2
        if i % 2 == 0:
            qk = _qk_proj(hn, ret_w_in, j, cos, sin, length // PROJ_ROW_TILE)
            v = _proj(hn, ret_w_in, j, 2 * RET_QK, RET_VDIM, BF16, name="ret_v_proj")
            gs = _proj(hn, ret_w_in, j, 2 * RET_QK + RET_VDIM, RET_VDIM, BF16, act="silu", name="ret_g_proj")
            y = _ret_scan(qk, v, gs, ret_gn_w[j], bsz, tpb)
            h, hn = _out_proj(y, ret_w_out, j, h, vmask, ffn_norm_w[i], BF16, name="ret_out_proj")
        else:
            xs, zs = _xz_proj(hn, ssd_w_in_t, j, ssd_conv_w[j], ssd_conv_b[j], vmask, length)
            bc = _bc_proj(hn, ssd_w_in_t, j, ssd_conv_w[j], ssd_conv_b[j], length)
            dt, cum = _dt_path(hn, ssd_w_in_t, j, ssd_dt_bias[j], ssd_a_log[j])
            y = _ssd_scan(xs, bc, zs, dt, cum, ssd_d[j], ssd_gnorm_w[j], bsz, tpb)
            h, hn = _out_proj(y, ssd_w_out, j, h, vmask, ffn_norm_w[i], BF16, name="ssd_out_proj")
        act = _ffn_up(hn, ffn_w_up, i, ffn_conv_w[i], ffn_conv_b[i], length)
        if i == depth - 1:
            (hn,) = _out_proj(act, ffn_w_down, i, h, vmask, final_norm_w, x.dtype,
                              name="ffn_down_final", keep_residual=False)
        else:
            next_w = ssd_norm_w[(i + 1) // 2] if (i + 1) % 2 == 1 else ret_norm_w[(i + 1) // 2]
            h, hn = _out_proj(act, ffn_w_down, i, h, vmask, next_w, BF16, name="ffn_down_proj")
    return hn.reshape(bsz, length, d)[:, META_PAD + N_META:]
```

```python
import functools

import jax
import jax.numpy as jnp
import numpy as np
from jax import lax
from jax.experimental import pallas as pl
from jax.experimental.pallas import tpu as pltpu

F32 = jnp.float32
BF16 = jnp.bfloat16

D_MODEL = 2048
CHUNK = 64
N_META = 16
META_PAD = CHUNK - N_META
NORM_EPS = 1e-6

RET_HEADS = 8
RET_CHUNK = 208
RET_DK = 256
RET_DV = 512
RET_QK = RET_HEADS * RET_DK
RET_VDIM = RET_HEADS * RET_DV
ROPE_BASE = 10000.0

SSD_DI = 4096
SSD_HEADDIM = 64
SSD_HEADS = 64
SSD_GROUPS = 8
SSD_HPG = 8
SSD_STATE = 128
SSD_BC = SSD_GROUPS * SSD_STATE
SSD_CONV_DIM = SSD_DI + 2 * SSD_BC
SSD_GW = SSD_HPG * SSD_HEADDIM
SPLIT_LANES = 256

FFN_DIM = 5632

V7X_VMEM_BYTES = 64 * 1024 * 1024
VMEM_LIMIT = V7X_VMEM_BYTES - 8 * 1024 * 1024

ROW_TILE = 832
RET_ROW_TILE = 4160
PROJ_ROW_TILE = 1040
COL_TILE = 512
WIDE_COL_TILE = 1024
W_CHUNK = 256
CARRY = 8
MXU_COLS = 256


def _params(sem):
    return pltpu.CompilerParams(dimension_semantics=sem, vmem_limit_bytes=VMEM_LIMIT)


def _sigmoid(x):
    return 1.0 / (1.0 + jnp.exp(-x))


def _silu(x):
    return x * _sigmoid(x)


def _softplus(x):
    return jnp.maximum(x, 0.0) + jnp.log1p(jnp.exp(-jnp.abs(x)))


def _dot(a, b):
    return jnp.dot(a, b, preferred_element_type=F32)


def _dot_nt(a, b):
    return lax.dot_general(a, b, (((1,), (1,)), ((), ())), preferred_element_type=F32)


def _dot_tn(a, b):
    return lax.dot_general(a, b, (((0,), (0,)), ((), ())), preferred_element_type=F32)


def _split3(x):
    hi = x.astype(BF16)
    r1 = x - hi.astype(F32)
    mid = r1.astype(BF16)
    lo = (r1 - mid.astype(F32)).astype(BF16)
    return hi, mid, lo


def _dot01(sel, x, sel_first):
    parts = _split3(x)
    if sel_first:
        return _dot(sel, parts[0]) + _dot(sel, parts[1]) + _dot(sel, parts[2])
    return _dot(parts[0], sel) + _dot(parts[1], sel) + _dot(parts[2], sel)


def _rms_scale(x, w):
    ms = jnp.mean(x * x, axis=-1, keepdims=True)
    return x * lax.rsqrt(ms + NORM_EPS) * w


def _cast_weight(w_ref, wb_ref):
    @pl.when(pl.program_id(1) == 0)
    def _():
        wb_ref[...] = w_ref[...].astype(BF16)


def _rmsnorm_kernel(x_ref, w_ref, o_ref):
    o_ref[...] = _rms_scale(x_ref[...], w_ref[...]).astype(o_ref.dtype)


def _rmsnorm(x, w):
    rows, d = x.shape
    return pl.pallas_call(
        _rmsnorm_kernel,
        grid=(rows // ROW_TILE,),
        in_specs=[pl.BlockSpec((ROW_TILE, d), lambda i: (i, 0)),
                  pl.BlockSpec((1, d), lambda i: (0, 0))],
        out_specs=pl.BlockSpec((ROW_TILE, d), lambda i: (i, 0)),
        out_shape=jax.ShapeDtypeStruct((rows, d), BF16),
        compiler_params=_params(("arbitrary",)),
        name="rmsnorm",
    )(x, w.reshape(1, d))


def _proj_kernel(a_ref, w_ref, o_ref, wb_ref, *, act):
    _cast_weight(w_ref, wb_ref)
    acc = _dot(a_ref[...], wb_ref[...])
    if act == "silu":
        acc = _silu(acc)
    o_ref[...] = acc.astype(o_ref.dtype)


def _proj(a, w, layer, col0, ncols, out_dtype, act=None, name="proj"):
    rows, k = a.shape
    off = col0 // WIDE_COL_TILE
    return pl.pallas_call(
        functools.partial(_proj_kernel, act=act),
        grid=(ncols // WIDE_COL_TILE, rows // PROJ_ROW_TILE),
        in_specs=[pl.BlockSpec((PROJ_ROW_TILE, k), lambda j, i: (i, 0)),
                  pl.BlockSpec((None, k, WIDE_COL_TILE), lambda j, i: (layer, 0, j + off))],
        out_specs=pl.BlockSpec((PROJ_ROW_TILE, WIDE_COL_TILE), lambda j, i: (i, j)),
        out_shape=jax.ShapeDtypeStruct((rows, ncols), out_dtype),
        scratch_shapes=[pltpu.VMEM((k, WIDE_COL_TILE), BF16)],
        compiler_params=_params(("arbitrary", "arbitrary")),
        name=name,
    )(a, w)


def _qk_kernel(a_ref, w_ref, cos_ref, sin_ref, o_ref, wb_ref, *, nq_tiles):
    _cast_weight(w_ref, wb_ref)
    acc = _dot(a_ref[...], wb_ref[...])
    scale = jnp.where(pl.program_id(0) >= nq_tiles, RET_DK ** -0.5, 1.0).astype(F32)
    cos = cos_ref[...] * scale
    sin = sin_ref[...] * scale
    half = RET_DK // 2
    for hd in range(WIDE_COL_TILE // RET_DK):
        lo = hd * RET_DK
        x1 = acc[:, lo:lo + half]
        x2 = acc[:, lo + half:lo + RET_DK]
        o_ref[:, lo:lo + half] = x1 * cos - x2 * sin
        o_ref[:, lo + half:lo + RET_DK] = x1 * sin + x2 * cos


def _qk_proj(a, w, layer, cos, sin, tiles_per_batch):
    rows, k = a.shape
    half = RET_DK // 2
    return pl.pallas_call(
        functools.partial(_qk_kernel, nq_tiles=RET_QK // WIDE_COL_TILE),
        grid=(2 * RET_QK // WIDE_COL_TILE, rows // PROJ_ROW_TILE),
        in_specs=[pl.BlockSpec((PROJ_ROW_TILE, k), lambda j, i: (i, 0)),
                  pl.BlockSpec((None, k, WIDE_COL_TILE), lambda j, i: (layer, 0, j)),
                  pl.BlockSpec((PROJ_ROW_TILE, half), lambda j, i: (i % tiles_per_batch, 0)),
                  pl.BlockSpec((PROJ_ROW_TILE, half), lambda j, i: (i % tiles_per_batch, 0))],
        out_specs=pl.BlockSpec((PROJ_ROW_TILE, WIDE_COL_TILE), lambda j, i: (i, j)),
        out_shape=jax.ShapeDtypeStruct((rows, 2 * RET_QK), F32),
        scratch_shapes=[pltpu.VMEM((k, WIDE_COL_TILE), BF16)],
        compiler_params=_params(("arbitrary", "arbitrary")),
        name="ret_qk_proj",
    )(a, w, cos, sin)


def _causal_conv(buf_ref, cw_ref, cb_ref, rows, width, cols=slice(None)):
    cw = cw_ref[:, cols]
    out = cb_ref[:, cols] + cw[width - 1:width, :] * buf_ref[CARRY:CARRY + rows, cols]
    for tap in range(width - 1):
        shift = width - 1 - tap
        out = out + cw[tap:tap + 1, :] * buf_ref[CARRY - shift:CARRY - shift + rows, cols]
    return out


def _conv_carry_reset(buf_ref, tiles_per_batch):
    @pl.when(pl.program_id(1) % tiles_per_batch == 0)
    def _():
        buf_ref[0:CARRY, :] = jnp.zeros((CARRY, buf_ref.shape[1]), F32)


def _conv_carry_save(buf_ref, rows):
    buf_ref[0:CARRY, :] = buf_ref[rows:rows + CARRY, :]


def _ffn_up_kernel(a_ref, wg_ref, wu_ref, cw_ref, cb_ref, o_ref, wgb_ref, wub_ref, buf_ref,
                   *, tiles_per_batch, width):
    _cast_weight(wg_ref, wgb_ref)
    _cast_weight(wu_ref, wub_ref)
    _conv_carry_reset(buf_ref, tiles_per_batch)
    rows = a_ref.shape[0]
    a = a_ref[...]
    buf_ref[CARRY:CARRY + rows, :] = _dot(a, wgb_ref[...])
    up = _dot(a, wub_ref[...])
    gate = _causal_conv(buf_ref, cw_ref, cb_ref, rows, width)
    o_ref[...] = (_silu(gate) * up).astype(o_ref.dtype)
    _conv_carry_save(buf_ref, rows)


def _ffn_up(a, w_up, layer, conv_w, conv_b, length):
    rows, k = a.shape
    width = conv_w.shape[0]
    nt = FFN_DIM // COL_TILE
    tm = PROJ_ROW_TILE
    assert length % tm == 0
    return pl.pallas_call(
        functools.partial(_ffn_up_kernel, tiles_per_batch=length // tm, width=width),
        grid=(nt, rows // tm),
        in_specs=[pl.BlockSpec((tm, k), lambda j, i: (i, 0)),
                  pl.BlockSpec((None, k, COL_TILE), lambda j, i: (layer, 0, j)),
                  pl.BlockSpec((None, k, COL_TILE), lambda j, i: (layer, 0, j + nt)),
                  pl.BlockSpec((width, COL_TILE), lambda j, i: (0, j)),
                  pl.BlockSpec((1, COL_TILE), lambda j, i: (0, j))],
        out_specs=pl.BlockSpec((tm, COL_TILE), lambda j, i: (i, j)),
        out_shape=jax.ShapeDtypeStruct((rows, FFN_DIM), BF16),
        scratch_shapes=[pltpu.VMEM((k, COL_TILE), BF16), pltpu.VMEM((k, COL_TILE), BF16),
                        pltpu.VMEM((CARRY + tm, COL_TILE), F32)],
        compiler_params=_params(("arbitrary", "arbitrary")),
        name="ffn_up",
    )(a, w_up, w_up, conv_w, conv_b.reshape(1, FFN_DIM))


def _xz_kernel(a_ref, wx_ref, wz_ref, cw_ref, cb_ref, m_ref, xo_ref, zo_ref, wxb_ref, wzb_ref, buf_ref,
               *, tiles_per_batch, width):
    _cast_weight(wx_ref, wxb_ref)
    _cast_weight(wz_ref, wzb_ref)
    _conv_carry_reset(buf_ref, tiles_per_batch)
    rows = a_ref.shape[0]
    a = a_ref[...]
    buf_ref[CARRY:CARRY + rows, :] = _dot_nt(a, wxb_ref[...])
    z = _dot_nt(a, wzb_ref[...])
    xo_ref[...] = _silu(_causal_conv(buf_ref, cw_ref, cb_ref, rows, width)) * m_ref[...]
    zo_ref[...] = _silu(z).astype(zo_ref.dtype)
    _conv_carry_save(buf_ref, rows)


def _xz_proj(a, w_in_t, layer, conv_w, conv_b, vmask, length):
    rows, k = a.shape
    width = conv_w.shape[0]
    tm, tn = PROJ_ROW_TILE, COL_TILE
    assert length % tm == 0
    x_off = SSD_DI // tn
    w_blk = lambda off: pl.BlockSpec((None, tn, k), lambda j, i: (layer, j + off, 0))
    out_blk = pl.BlockSpec((tm, tn), lambda j, i: (i, j))
    shapes = [jax.ShapeDtypeStruct((rows, SSD_DI), F32), jax.ShapeDtypeStruct((rows, SSD_DI), BF16)]
    return pl.pallas_call(
        functools.partial(_xz_kernel, tiles_per_batch=length // tm, width=width),
        grid=(SSD_DI // tn, rows // tm),
        in_specs=[pl.BlockSpec((tm, k), lambda j, i: (i, 0)),
                  w_blk(x_off), w_blk(0),
                  pl.BlockSpec((width, tn), lambda j, i: (0, j)),
                  pl.BlockSpec((1, tn), lambda j, i: (0, j)),
                  pl.BlockSpec((tm, 1), lambda j, i: (i, 0))],
        out_specs=[out_blk, out_blk],
        out_shape=shapes,
        scratch_shapes=[pltpu.VMEM((tn, k), BF16), pltpu.VMEM((tn, k), BF16),
                        pltpu.VMEM((CARRY + tm, tn), F32)],
        compiler_params=_params(("arbitrary", "arbitrary")),
        name="ssd_xz_proj",
    )(a, w_in_t, w_in_t, conv_w, conv_b.reshape(1, SSD_CONV_DIM), vmask)


def _bc_kernel(a_ref, w_ref, cw_ref, cb_ref, o_ref, wb_ref, buf_ref, *, tiles_per_batch, width):
    _cast_weight(w_ref, wb_ref)
    _conv_carry_reset(buf_ref, tiles_per_batch)
    rows = a_ref.shape[0]
    a = a_ref[...]
    for lo in range(0, o_ref.shape[1], 2 * MXU_COLS):
        cols = slice(lo, lo + 2 * MXU_COLS)
        buf_ref[CARRY:CARRY + rows, cols] = _dot_nt(a, wb_ref[cols, :])
        o_ref[:, cols] = _silu(_causal_conv(buf_ref, cw_ref, cb_ref, rows, width, cols)).astype(o_ref.dtype)
    _conv_carry_save(buf_ref, rows)


def _bc_proj(a, w_in_t, layer, conv_w, conv_b, length):
    rows, k = a.shape
    width = conv_w.shape[0]
    tm, tn = PROJ_ROW_TILE, WIDE_COL_TILE
    assert length % tm == 0
    conv_off = SSD_DI // tn
    w_off = 2 * SSD_DI // tn
    return pl.pallas_call(
        functools.partial(_bc_kernel, tiles_per_batch=length // tm, width=width),
        grid=(2 * SSD_BC // tn, rows // tm),
        in_specs=[pl.BlockSpec((tm, k), lambda j, i: (i, 0)),
                  pl.BlockSpec((None, tn, k), lambda j, i: (layer, j + w_off, 0)),
                  pl.BlockSpec((width, tn), lambda j, i: (0, j + conv_off)),
                  pl.BlockSpec((1, tn), lambda j, i: (0, j + conv_off))],
        out_specs=pl.BlockSpec((tm, tn), lambda j, i: (i, j)),
        out_shape=jax.ShapeDtypeStruct((rows, 2 * SSD_BC), BF16),
        scratch_shapes=[pltpu.VMEM((tn, k), BF16), pltpu.VMEM((CARRY + tm, tn), F32)],
        compiler_params=_params(("arbitrary", "arbitrary")),
        name="ssd_bc_proj",
    )(a, w_in_t, conv_w, conv_b.reshape(1, SSD_CONV_DIM))


def _stack3(x, place_ref):
    parts = _split3(x)
    out = _dot(parts[0], place_ref[0]) + _dot(parts[1], place_ref[1]) + _dot(parts[2], place_ref[2])
    return out.astype(BF16)


def _dt_kernel(a_ref, w_ref, bias_ref, alog_ref, tri_ref, place_ref, dt_ref, cum_ref):
    dt = _softplus(_dot_nt(a_ref[...], w_ref[...].astype(BF16)) + bias_ref[...])
    da = dt * (-jnp.exp(alog_ref[...]))
    cum = _dot01(tri_ref[...], da, sel_first=True)
    dt_ref[...] = _stack3(dt, place_ref)
    cum_ref[...] = _stack3(cum, place_ref)


def _dt_path(a, w_in_t, layer, dt_bias, a_log):
    rows, k = a.shape
    dt_row_blk = (SSD_DI + SSD_CONV_DIM) // SSD_HEADS
    assert dt_row_blk * SSD_HEADS == SSD_DI + SSD_CONV_DIM
    r = np.arange(ROW_TILE)
    tri = ((r[:, None] // CHUNK == r[None, :] // CHUNK) & (r[:, None] >= r[None, :]))
    tri = jnp.asarray(tri, BF16)
    place = np.zeros((3, SSD_HEADS, SPLIT_LANES), np.float32)
    for term in range(3):
        place[term, np.arange(SSD_HEADS), term * SSD_HEADS + np.arange(SSD_HEADS)] = 1.0
    place = jnp.asarray(place, BF16)
    shp = jax.ShapeDtypeStruct((rows, SPLIT_LANES), BF16)
    return pl.pallas_call(
        _dt_kernel,
        grid=(rows // ROW_TILE,),
        in_specs=[pl.BlockSpec((ROW_TILE, k), lambda i: (i, 0)),
                  pl.BlockSpec((None, SSD_HEADS, k), lambda i: (layer, dt_row_blk, 0)),
                  pl.BlockSpec((1, SSD_HEADS), lambda i: (0, 0)),
                  pl.BlockSpec((1, SSD_HEADS), lambda i: (0, 0)),
                  pl.BlockSpec((ROW_TILE, ROW_TILE), lambda i: (0, 0)),
                  pl.BlockSpec((3, SSD_HEADS, SPLIT_LANES), lambda i: (0, 0, 0))],
        out_specs=[pl.BlockSpec((ROW_TILE, SPLIT_LANES), lambda i: (i, 0)),
                   pl.BlockSpec((ROW_TILE, SPLIT_LANES), lambda i: (i, 0))],
        out_shape=[shp, shp],
        compiler_params=_params(("arbitrary",)),
        name="ssd_dt_path",
    )(a, w_in_t, dt_bias.reshape(1, SSD_HEADS), a_log.reshape(1, SSD_HEADS), tri, place)


def _out_kernel(a_ref, w_ref, h_ref, m_ref, nw_ref, *refs, n_load, keep_residual):
    if keep_residual:
        ho_ref, hn_ref, wb_ref = refs
    else:
        hn_ref, wb_ref = refs
    step = pl.program_id(0)

    @pl.when(step < n_load)
    def _():
        r0 = pl.multiple_of(step * W_CHUNK, W_CHUNK)
        wb_ref[pl.ds(r0, W_CHUNK), :] = w_ref[...].astype(BF16)

    @pl.when(step >= n_load)
    def _():
        hnew = (h_ref[...] + _dot(a_ref[...], wb_ref[...])) * m_ref[...]
        if keep_residual:
            ho_ref[...] = hnew
        hn_ref[...] = _rms_scale(hnew, nw_ref[...]).astype(hn_ref.dtype)


def _out_row_tile(k):
    return 416 if k <= 4096 else 320


def _out_proj(a, w, layer, h, vmask, norm_w, hn_dtype, name, keep_residual=True):
    rows, k = a.shape
    d = w.shape[2]
    tm = _out_row_tile(k)
    n_load = k // W_CHUNK
    assert n_load * W_CHUNK == k and rows % tm == 0

    def row(step):
        return jnp.maximum(step - n_load, 0)

    row_blk = pl.BlockSpec((tm, d), lambda s: (row(s), 0))
    hn_shape = jax.ShapeDtypeStruct((rows, d), hn_dtype)
    if keep_residual:
        out_specs, out_shape = [row_blk, row_blk], [jax.ShapeDtypeStruct((rows, d), F32), hn_shape]
    else:
        out_specs, out_shape = [row_blk], [hn_shape]
    return pl.pallas_call(
        functools.partial(_out_kernel, n_load=n_load, keep_residual=keep_residual),
        grid=(n_load + rows // tm,),
        in_specs=[pl.BlockSpec((tm, k), lambda s: (row(s), 0)),
                  pl.BlockSpec((None, W_CHUNK, d), lambda s: (layer, jnp.minimum(s, n_load - 1), 0)),
                  row_blk,
                  pl.BlockSpec((tm, 1), lambda s: (row(s), 0)),
                  pl.BlockSpec((1, d), lambda s: (0, 0))],
        out_specs=out_specs,
        out_shape=out_shape,
        scratch_shapes=[pltpu.VMEM((k, d), BF16)],
        compiler_params=_params(("arbitrary",)),
        name=name,
    )(a, w, h, vmask, norm_w.reshape(1, d))


def _ret_scan_kernel(q_ref, k_ref, v_ref, gs_ref, gnw_ref, intra_ref, qd_ref, kd_ref, cd_ref,
                     y_ref, state_ref, *, chunks):
    @pl.when(pl.program_id(2) == 0)
    def _():
        state_ref[...] = jnp.zeros(state_ref.shape, F32)

    intra = intra_ref[0]
    q_decay = qd_ref[0]
    k_decay = kd_ref[0]
    chunk_decay = cd_ref[0]
    gnw = gnw_ref[...]

    def body(c, carry):
        r0 = c * RET_CHUNK
        q = q_ref[pl.ds(r0, RET_CHUNK), :]
        k = k_ref[pl.ds(r0, RET_CHUNK), :]
        v = v_ref[pl.ds(r0, RET_CHUNK), :]
        qb = q.astype(BF16)
        scores = _dot_nt(qb, k.astype(BF16)) * intra
        o = _dot(scores.astype(BF16), v)
        state = state_ref[...]
        o = o + q_decay * _dot(qb, state.astype(BF16))
        kd = (k * k_decay).astype(BF16)
        state_ref[...] = state * chunk_decay + _dot_tn(kd, v)
        o = o * lax.rsqrt(jnp.mean(o * o, axis=-1, keepdims=True) + NORM_EPS) * gnw
        y_ref[pl.ds(r0, RET_CHUNK), :] = (gs_ref[pl.ds(r0, RET_CHUNK), :] * o).astype(y_ref.dtype)
        return carry

    for c in range(chunks):
        body(c, 0)


def _ret_scan(qk, v, gs, gn_w, bsz, length):
    rows = qk.shape[0]
    log_gamma = np.log1p(-np.exp2(-5.0 - np.arange(RET_HEADS, dtype=np.float64)))
    chunk = RET_CHUNK
    tm = RET_ROW_TILE
    assert tm % chunk == 0 and length % tm == 0
    idx = np.arange(chunk, dtype=np.float64)
    diff = idx[:, None] - idx[None, :]
    intra = np.where(diff[None] >= 0, np.exp(np.maximum(diff, 0.0)[None] * log_gamma[:, None, None]), 0.0)
    q_decay = np.exp((idx + 1.0)[None, :] * log_gamma[:, None])[..., None]
    k_decay = np.exp((chunk - 1.0 - idx)[None, :] * log_gamma[:, None])[..., None]
    chunk_decay = np.exp(chunk * log_gamma)[:, None, None]
    intra, q_decay, k_decay, chunk_decay = (jnp.asarray(t, F32) for t in (intra, q_decay, k_decay, chunk_decay))
    tpb = length // tm
    nkh = RET_QK // RET_DK
    return pl.pallas_call(
        functools.partial(_ret_scan_kernel, chunks=tm // chunk),
        grid=(bsz, RET_HEADS, tpb),
        in_specs=[pl.BlockSpec((tm, RET_DK), lambda b, h, t: (b * tpb + t, h)),
                  pl.BlockSpec((tm, RET_DK), lambda b, h, t: (b * tpb + t, nkh + h)),
                  pl.BlockSpec((tm, RET_DV), lambda b, h, t: (b * tpb + t, h)),
                  pl.BlockSpec((tm, RET_DV), lambda b, h, t: (b * tpb + t, h)),
                  pl.BlockSpec((1, RET_DV), lambda b, h, t: (0, h)),
                  pl.BlockSpec((1, chunk, chunk), lambda b, h, t: (h, 0, 0)),
                  pl.BlockSpec((1, chunk, 1), lambda b, h, t: (h, 0, 0)),
                  pl.BlockSpec((1, chunk, 1), lambda b, h, t: (h, 0, 0)),
                  pl.BlockSpec((1, 1, 1), lambda b, h, t: (h, 0, 0))],
        out_specs=pl.BlockSpec((tm, RET_DV), lambda b, h, t: (b * tpb + t, h)),
        out_shape=jax.ShapeDtypeStruct((rows, RET_VDIM), BF16),
        scratch_shapes=[pltpu.VMEM((RET_DK, RET_DV), F32)],
        compiler_params=_params(("arbitrary", "arbitrary", "arbitrary")),
        name="ret_scan",
    )(qk, qk, v, gs, gn_w.reshape(1, RET_VDIM), intra, q_decay, k_decay, chunk_decay)


def _ssd_scan_kernel(x_ref, b_ref, c_ref, z_ref, dt_ref, cum_ref, sel_ref, d_ref, gw_ref,
                     y_ref, state_ref, dte_ref, cume_ref, *, chunks):
    @pl.when(pl.program_id(2) == 0)
    def _():
        state_ref[...] = jnp.zeros(state_ref.shape, F32)

    sel = sel_ref[...]
    dte_ref[...] = _dot(dt_ref[...], sel)
    cume_ref[...] = _dot(cum_ref[...], sel)

    gwid = SSD_GW
    row = lax.broadcasted_iota(jnp.int32, (CHUNK, gwid), 0)
    lane_pos = lax.broadcasted_iota(jnp.int32, (CHUNK, gwid), 1) & (CHUNK - 1)
    causal = row >= lane_pos
    diag = row == lane_pos
    quad = 4 * SSD_HEADDIM
    blk_r = lax.broadcasted_iota(jnp.int32, (quad, quad), 0) // SSD_HEADDIM
    blk_c = lax.broadcasted_iota(jnp.int32, (quad, quad), 1) // SSD_HEADDIM
    blockdiag = blk_r == blk_c
    d_skip = d_ref[...]
    gw = gw_ref[...]

    def body(c, carry):
        r0 = c * CHUNK
        rows = pl.ds(r0, CHUNK)
        x = x_ref[rows, :]
        cum = cume_ref[rows, :]
        bb = b_ref[rows, :].astype(BF16)
        cb = c_ref[rows, :].astype(BF16)
        xdt = x * dte_ref[rows, :]
        cum_row = jnp.sum(jnp.where(diag, cum, 0.0), axis=0, keepdims=True)
        decay = jnp.where(causal, jnp.exp(cum - cum_row), 0.0)
        gram = _dot_nt(cb, jnp.concatenate([bb] * SSD_HPG, axis=0))
        attn = (gram * decay).astype(BF16)
        xdt_b = xdt.astype(BF16)
        ys = []
        for qd in range(gwid // quad):
            xq = xdt_b[:, qd * quad:(qd + 1) * quad]
            rhs = jnp.where(blockdiag, jnp.concatenate([xq] * 4, axis=0), jnp.zeros((), BF16))
            ys.append(_dot(attn[:, qd * quad:(qd + 1) * quad], rhs))
        y = jnp.concatenate(ys, axis=1)
        state = state_ref[...]
        y = y + _dot(cb, state.astype(BF16)) * jnp.exp(cum)
        cum_last = cum[CHUNK - 1:CHUNK, :]
        xw = (xdt * jnp.exp(cum_last - cum)).astype(BF16)
        state_ref[...] = state * jnp.exp(cum_last) + _dot_tn(bb, xw)
        y = (y + x * d_skip) * z_ref[rows, :]
        y = y * lax.rsqrt(jnp.mean(y * y, axis=-1, keepdims=True) + NORM_EPS) * gw
        y_ref[rows, :] = y.astype(y_ref.dtype)
        return carry

    for c in range(chunks):
        body(c, 0)


def _ssd_scan(xs, bc, zs, dt, cum, d_skip, gnorm_w, bsz, tiles_per_batch):
    rows = xs.shape[0]
    tpb = tiles_per_batch
    b_blk = 0
    c_blk = SSD_BC // SSD_STATE
    d_exp = jnp.repeat(d_skip, SSD_HEADDIM).reshape(1, SSD_DI)
    one_hot = np.arange(SSD_HEADS)[:, None] == (np.arange(SSD_DI)[None, :] // SSD_HEADDIM)
    expand = np.zeros((SPLIT_LANES, SSD_DI), np.float32)
    expand[:3 * SSD_HEADS] = np.tile(one_hot, (3, 1))
    expand = jnp.asarray(expand, BF16)
    wide = pl.BlockSpec((ROW_TILE, SSD_GW), lambda b, g, t: (b * tpb + t, g))
    heads = pl.BlockSpec((ROW_TILE, SPLIT_LANES), lambda b, g, t: (b * tpb + t, 0))
    return pl.pallas_call(
        functools.partial(_ssd_scan_kernel, chunks=ROW_TILE // CHUNK),
        grid=(bsz, SSD_GROUPS, tpb),
        in_specs=[wide,
                  pl.BlockSpec((ROW_TILE, SSD_STATE), lambda b, g, t: (b * tpb + t, b_blk + g)),
                  pl.BlockSpec((ROW_TILE, SSD_STATE), lambda b, g, t: (b * tpb + t, c_blk + g)),
                  wide, heads, heads,
                  pl.BlockSpec((SPLIT_LANES, SSD_GW), lambda b, g, t: (0, g)),
                  pl.BlockSpec((1, SSD_GW), lambda b, g, t: (0, g)),
                  pl.BlockSpec((1, SSD_GW), lambda b, g, t: (0, g))],
        out_specs=wide,
        out_shape=jax.ShapeDtypeStruct((rows, SSD_DI), BF16),
        scratch_shapes=[pltpu.VMEM((SSD_STATE, SSD_GW), F32),
                        pltpu.VMEM((ROW_TILE, SSD_GW), F32), pltpu.VMEM((ROW_TILE, SSD_GW), F32)],
        compiler_params=_params(("arbitrary", "arbitrary", "arbitrary")),
        name="ssd_scan",
    )(xs, bc, bc, zs, dt, cum, expand, d_exp, gnorm_w.reshape(1, SSD_DI))


def kernel(x, meta_tokens, ret_norm_w, ret_w_in, ret_gn_w, ret_w_out, ssd_norm_w, ssd_w_in, ssd_conv_w, ssd_conv_b, ssd_dt_bias, ssd_a_log, ssd_d, ssd_gnorm_w, ssd_w_out, ffn_norm_w, ffn_w_up, ffn_conv_w, ffn_conv_b, ffn_w_down, final_norm_w):
    bsz, seq, d = x.shape
    length = META_PAD + N_META + seq
    assert d == D_MODEL and length % ROW_TILE == 0 and ROW_TILE % CHUNK == 0 and length % PROJ_ROW_TILE == 0
    tpb = length // ROW_TILE
    depth = ffn_w_up.shape[0]

    meta = jnp.broadcast_to(meta_tokens.astype(x.dtype)[None], (bsz, N_META, d))
    h = jnp.concatenate([jnp.zeros((bsz, META_PAD, d), x.dtype), meta, x], axis=1).reshape(bsz * length, d)

    pos_i = np.arange(length) - META_PAD
    vmask = jnp.asarray(np.tile((pos_i >= 0).astype(np.float32), bsz).reshape(bsz * length, 1))
    half = RET_DK // 2
    inv = ROPE_BASE ** (-jnp.arange(half, dtype=F32) / half)
    ang = jnp.asarray(pos_i, F32)[:, None] * inv[None, :]
    cos, sin = jnp.cos(ang), jnp.sin(ang)

    ssd_w_in_t = jnp.swapaxes(ssd_w_in, 1, 2)
    hn = _rmsnorm(h, ret_norm_w[0])
    for i in range(depth):
        j = i // 2
        if i % 2 == 0:
            qk = _qk_proj(hn, ret_w_in, j, cos, sin, length // PROJ_ROW_TILE)
            v = _proj(hn, ret_w_in, j, 2 * RET_QK, RET_VDIM, BF16, name="ret_v_proj")
            gs = _proj(hn, ret_w_in, j, 2 * RET_QK + RET_VDIM, RET_VDIM, BF16, act="silu", name="ret_g_proj")
            y = _ret_scan(qk, v, gs, ret_gn_w[j], bsz, length)
            h, hn = _out_proj(y, ret_w_out, j, h, vmask, ffn_norm_w[i], BF16, name="ret_out_proj")
        else:
            xs, zs = _xz_proj(hn, ssd_w_in_t, j, ssd_conv_w[j], ssd_conv_b[j], vmask, length)
            bc = _bc_proj(hn, ssd_w_in_t, j, ssd_conv_w[j], ssd_conv_b[j], length)
            dt, cum = _dt_path(hn, ssd_w_in_t, j, ssd_dt_bias[j], ssd_a_log[j])
            y = _ssd_scan(xs, bc, zs, dt, cum, ssd_d[j], ssd_gnorm_w[j], bsz, tpb)
            h, hn = _out_proj(y, ssd_w_out, j, h, vmask, ffn_norm_w[i], BF16, name="ssd_out_proj")
        act = _ffn_up(hn, ffn_w_up, i, ffn_conv_w[i], ffn_conv_b[i], length)
        if i == depth - 1:
            (hn,) = _out_proj(act, ffn_w_down, i, h, vmask, final_norm_w, x.dtype,
                              name="ffn_down_final", keep_residual=False)
        else:
            next_w = ssd_norm_w[(i + 1) // 2] if (i + 1) % 2 == 1 else ret_norm_w[(i + 1) // 2]
            h, hn = _out_proj(act, ffn_w_down, i, h, vmask, next_w, BF16, name="ffn_down_proj")
    return hn.reshape(bsz, length, d)[:, META_PAD + N_META:]
```

```python
import functools

import jax
import jax.numpy as jnp
import numpy as np
from jax import lax
from jax.experimental import pallas as pl
from jax.experimental.pallas import tpu as pltpu

F32 = jnp.float32
BF16 = jnp.bfloat16

D_MODEL = 2048
CHUNK = 64
N_META = 16
META_PAD = CHUNK - N_META
NORM_EPS = 1e-6

RET_HEADS = 8
RET_CHUNK = 208
RET_DK = 256
RET_DV = 512
RET_QK = RET_HEADS * RET_DK
RET_VDIM = RET_HEADS * RET_DV
ROPE_BASE = 10000.0

SSD_DI = 4096
SSD_HEADDIM = 64
SSD_HEADS = 64
SSD_GROUPS = 8
SSD_HPG = 8
SSD_STATE = 128
SSD_BC = SSD_GROUPS * SSD_STATE
SSD_CONV_DIM = SSD_DI + 2 * SSD_BC
SSD_GW = SSD_HPG * SSD_HEADDIM
SPLIT_LANES = 256

FFN_DIM = 5632

V7X_VMEM_BYTES = 64 * 1024 * 1024
VMEM_LIMIT = V7X_VMEM_BYTES - 8 * 1024 * 1024

ROW_TILE = 832
RET_ROW_TILE = 4160
PROJ_ROW_TILE = 1040
COL_TILE = 512
WIDE_COL_TILE = 1024
W_CHUNK = 512
CARRY = 8
MXU_COLS = 256


def _params(sem):
    return pltpu.CompilerParams(dimension_semantics=sem, vmem_limit_bytes=VMEM_LIMIT)


def _sigmoid(x):
    return 1.0 / (1.0 + jnp.exp(-x))


def _silu(x):
    return x * _sigmoid(x)


def _softplus(x):
    return jnp.maximum(x, 0.0) + jnp.log1p(jnp.exp(-jnp.abs(x)))


def _dot(a, b):
    return jnp.dot(a, b, preferred_element_type=F32)


def _dot_nt(a, b):
    return lax.dot_general(a, b, (((1,), (1,)), ((), ())), preferred_element_type=F32)


def _dot_tn(a, b):
    return lax.dot_general(a, b, (((0,), (0,)), ((), ())), preferred_element_type=F32)


def _split3(x):
    hi = x.astype(BF16)
    r1 = x - hi.astype(F32)
    mid = r1.astype(BF16)
    lo = (r1 - mid.astype(F32)).astype(BF16)
    return hi, mid, lo


def _dot01(sel, x, sel_first):
    parts = _split3(x)
    if sel_first:
        return _dot(sel, parts[0]) + _dot(sel, parts[1]) + _dot(sel, parts[2])
    return _dot(parts[0], sel) + _dot(parts[1], sel) + _dot(parts[2], sel)


def _rms_scale(x, w):
    ms = jnp.mean(x * x, axis=-1, keepdims=True)
    return x * lax.rsqrt(ms + NORM_EPS) * w


def _cast_weight(w_ref, wb_ref):
    @pl.when(pl.program_id(1) == 0)
    def _():
        wb_ref[...] = w_ref[...].astype(BF16)


def _rmsnorm_kernel(x_ref, w_ref, o_ref):
    o_ref[...] = _rms_scale(x_ref[...], w_ref[...]).astype(o_ref.dtype)


def _rmsnorm(x, w):
    rows, d = x.shape
    return pl.pallas_call(
        _rmsnorm_kernel,
        grid=(rows // ROW_TILE,),
        in_specs=[pl.BlockSpec((ROW_TILE, d), lambda i: (i, 0)),
                  pl.BlockSpec((1, d), lambda i: (0, 0))],
        out_specs=pl.BlockSpec((ROW_TILE, d), lambda i: (i, 0)),
        out_shape=jax.ShapeDtypeStruct((rows, d), BF16),
        compiler_params=_params(("arbitrary",)),
        name="rmsnorm",
    )(x, w.reshape(1, d))


def _proj_kernel(a_ref, w_ref, o_ref, wb_ref, *, act):
    _cast_weight(w_ref, wb_ref)
    acc = _dot(a_ref[...], wb_ref[...])
    if act == "silu":
        acc = _silu(acc)
    o_ref[...] = acc.astype(o_ref.dtype)


def _proj(a, w, layer, col0, ncols, out_dtype, act=None, name="proj"):
    rows, k = a.shape
    off = col0 // WIDE_COL_TILE
    return pl.pallas_call(
        functools.partial(_proj_kernel, act=act),
        grid=(ncols // WIDE_COL_TILE, rows // PROJ_ROW_TILE),
        in_specs=[pl.BlockSpec((PROJ_ROW_TILE, k), lambda j, i: (i, 0)),
                  pl.BlockSpec((None, k, WIDE_COL_TILE), lambda j, i: (layer, 0, j + off))],
        out_specs=pl.BlockSpec((PROJ_ROW_TILE, WIDE_COL_TILE), lambda j, i: (i, j)),
        out_shape=jax.ShapeDtypeStruct((rows, ncols), out_dtype),
        scratch_shapes=[pltpu.VMEM((k, WIDE_COL_TILE), BF16)],
        compiler_params=_params(("arbitrary", "arbitrary")),
        name=name,
    )(a, w)


def _qk_kernel(a_ref, w_ref, cos_ref, sin_ref, o_ref, wb_ref, *, nq_tiles):
    _cast_weight(w_ref, wb_ref)
    acc = _dot(a_ref[...], wb_ref[...])
    scale = jnp.where(pl.program_id(0) >= nq_tiles, RET_DK ** -0.5, 1.0).astype(F32)
    cos = cos_ref[...] * scale
    sin = sin_ref[...] * scale
    half = RET_DK // 2
    for hd in range(WIDE_COL_TILE // RET_DK):
        lo = hd * RET_DK
        x1 = acc[:, lo:lo + half]
        x2 = acc[:, lo + half:lo + RET_DK]
        o_ref[:, lo:lo + half] = x1 * cos - x2 * sin
        o_ref[:, lo + half:lo + RET_DK] = x1 * sin + x2 * cos


def _qk_proj(a, w, layer, cos, sin, tiles_per_batch):
    rows, k = a.shape
    half = RET_DK // 2
    return pl.pallas_call(
        functools.partial(_qk_kernel, nq_tiles=RET_QK // WIDE_COL_TILE),
        grid=(2 * RET_QK // WIDE_COL_TILE, rows // PROJ_ROW_TILE),
        in_specs=[pl.BlockSpec((PROJ_ROW_TILE, k), lambda j, i: (i, 0)),
                  pl.BlockSpec((None, k, WIDE_COL_TILE), lambda j, i: (layer, 0, j)),
                  pl.BlockSpec((PROJ_ROW_TILE, half), lambda j, i: (i % tiles_per_batch, 0)),
                  pl.BlockSpec((PROJ_ROW_TILE, half), lambda j, i: (i % tiles_per_batch, 0))],
        out_specs=pl.BlockSpec((PROJ_ROW_TILE, WIDE_COL_TILE), lambda j, i: (i, j)),
        out_shape=jax.ShapeDtypeStruct((rows, 2 * RET_QK), F32),
        scratch_shapes=[pltpu.VMEM((k, WIDE_COL_TILE), BF16)],
        compiler_params=_params(("arbitrary", "arbitrary")),
        name="ret_qk_proj",
    )(a, w, cos, sin)


def _causal_conv(buf_ref, cw_ref, cb_ref, rows, width, cols=slice(None)):
    cw = cw_ref[:, cols]
    out = cb_ref[:, cols] + cw[width - 1:width, :] * buf_ref[CARRY:CARRY + rows, cols]
    for tap in range(width - 1):
        shift = width - 1 - tap
        out = out + cw[tap:tap + 1, :] * buf_ref[CARRY - shift:CARRY - shift + rows, cols]
    return out


def _conv_carry_reset(buf_ref, tiles_per_batch):
    @pl.when(pl.program_id(1) % tiles_per_batch == 0)
    def _():
        buf_ref[0:CARRY, :] = jnp.zeros((CARRY, buf_ref.shape[1]), F32)


def _conv_carry_save(buf_ref, rows):
    buf_ref[0:CARRY, :] = buf_ref[rows:rows + CARRY, :]


def _ffn_up_kernel(a_ref, wg_ref, wu_ref, cw_ref, cb_ref, o_ref, wgb_ref, wub_ref, buf_ref,
                   *, tiles_per_batch, width):
    _cast_weight(wg_ref, wgb_ref)
    _cast_weight(wu_ref, wub_ref)
    _conv_carry_reset(buf_ref, tiles_per_batch)
    rows = a_ref.shape[0]
    a = a_ref[...]
    buf_ref[CARRY:CARRY + rows, :] = _dot(a, wgb_ref[...])
    up = _dot(a, wub_ref[...])
    gate = _causal_conv(buf_ref, cw_ref, cb_ref, rows, width)
    o_ref[...] = (_silu(gate) * up).astype(o_ref.dtype)
    _conv_carry_save(buf_ref, rows)


def _ffn_up(a, w_up, layer, conv_w, conv_b, length):
    rows, k = a.shape
    width = conv_w.shape[0]
    nt = FFN_DIM // COL_TILE
    tm = PROJ_ROW_TILE
    assert length % tm == 0
    return pl.pallas_call(
        functools.partial(_ffn_up_kernel, tiles_per_batch=length // tm, width=width),
        grid=(nt, rows // tm),
        in_specs=[pl.BlockSpec((tm, k), lambda j, i: (i, 0)),
                  pl.BlockSpec((None, k, COL_TILE), lambda j, i: (layer, 0, j)),
                  pl.BlockSpec((None, k, COL_TILE), lambda j, i: (layer, 0, j + nt)),
                  pl.BlockSpec((width, COL_TILE), lambda j, i: (0, j)),
                  pl.BlockSpec((1, COL_TILE), lambda j, i: (0, j))],
        out_specs=pl.BlockSpec((tm, COL_TILE), lambda j, i: (i, j)),
        out_shape=jax.ShapeDtypeStruct((rows, FFN_DIM), BF16),
        scratch_shapes=[pltpu.VMEM((k, COL_TILE), BF16), pltpu.VMEM((k, COL_TILE), BF16),
                        pltpu.VMEM((CARRY + tm, COL_TILE), F32)],
        compiler_params=_params(("arbitrary", "arbitrary")),
        name="ffn_up",
    )(a, w_up, w_up, conv_w, conv_b.reshape(1, FFN_DIM))


def _xz_kernel(a_ref, wx_ref, wz_ref, cw_ref, cb_ref, m_ref, xo_ref, zo_ref, wxb_ref, wzb_ref, buf_ref,
               *, tiles_per_batch, width):
    _cast_weight(wx_ref, wxb_ref)
    _cast_weight(wz_ref, wzb_ref)
    _conv_carry_reset(buf_ref, tiles_per_batch)
    rows = a_ref.shape[0]
    a = a_ref[...]
    buf_ref[CARRY:CARRY + rows, :] = _dot_nt(a, wxb_ref[...])
    z = _dot_nt(a, wzb_ref[...])
    xo_ref[...] = _silu(_causal_conv(buf_ref, cw_ref, cb_ref, rows, width)) * m_ref[...]
    zo_ref[...] = _silu(z).astype(zo_ref.dtype)
    _conv_carry_save(buf_ref, rows)


def _xz_proj(a, w_in_t, layer, conv_w, conv_b, vmask, length):
    rows, k = a.shape
    width = conv_w.shape[0]
    tm, tn = PROJ_ROW_TILE, COL_TILE
    assert length % tm == 0
    x_off = SSD_DI // tn
    w_blk = lambda off: pl.BlockSpec((None, tn, k), lambda j, i: (layer, j + off, 0))
    out_blk = pl.BlockSpec((tm, tn), lambda j, i: (i, j))
    shapes = [jax.ShapeDtypeStruct((rows, SSD_DI), F32), jax.ShapeDtypeStruct((rows, SSD_DI), BF16)]
    return pl.pallas_call(
        functools.partial(_xz_kernel, tiles_per_batch=length // tm, width=width),
        grid=(SSD_DI // tn, rows // tm),
        in_specs=[pl.BlockSpec((tm, k), lambda j, i: (i, 0)),
                  w_blk(x_off), w_blk(0),
                  pl.BlockSpec((width, tn), lambda j, i: (0, j)),
                  pl.BlockSpec((1, tn), lambda j, i: (0, j)),
                  pl.BlockSpec((tm, 1), lambda j, i: (i, 0))],
        out_specs=[out_blk, out_blk],
        out_shape=shapes,
        scratch_shapes=[pltpu.VMEM((tn, k), BF16), pltpu.VMEM((tn, k), BF16),
                        pltpu.VMEM((CARRY + tm, tn), F32)],
        compiler_params=_params(("arbitrary", "arbitrary")),
        name="ssd_xz_proj",
    )(a, w_in_t, w_in_t, conv_w, conv_b.reshape(1, SSD_CONV_DIM), vmask)


def _bc_kernel(a_ref, w_ref, cw_ref, cb_ref, o_ref, wb_ref, buf_ref, *, tiles_per_batch, width):
    _cast_weight(w_ref, wb_ref)
    _conv_carry_reset(buf_ref, tiles_per_batch)
    rows = a_ref.shape[0]
    a = a_ref[...]
    for lo in range(0, o_ref.shape[1], 2 * MXU_COLS):
        cols = slice(lo, lo + 2 * MXU_COLS)
        buf_ref[CARRY:CARRY + rows, cols] = _dot_nt(a, wb_ref[cols, :])
        o_ref[:, cols] = _silu(_causal_conv(buf_ref, cw_ref, cb_ref, rows, width, cols)).astype(o_ref.dtype)
    _conv_carry_save(buf_ref, rows)


def _bc_proj(a, w_in_t, layer, conv_w, conv_b, length):
    rows, k = a.shape
    width = conv_w.shape[0]
    tm, tn = PROJ_ROW_TILE, WIDE_COL_TILE
    assert length % tm == 0
    conv_off = SSD_DI // tn
    w_off = 2 * SSD_DI // tn
    return pl.pallas_call(
        functools.partial(_bc_kernel, tiles_per_batch=length // tm, width=width),
        grid=(2 * SSD_BC // tn, rows // tm),
        in_specs=[pl.BlockSpec((tm, k), lambda j, i: (i, 0)),
                  pl.BlockSpec((None, tn, k), lambda j, i: (layer, j + w_off, 0)),
                  pl.BlockSpec((width, tn), lambda j, i: (0, j + conv_off)),
                  pl.BlockSpec((1, tn), lambda j, i: (0, j + conv_off))],
        out_specs=pl.BlockSpec((tm, tn), lambda j, i: (i, j)),
        out_shape=jax.ShapeDtypeStruct((rows, 2 * SSD_BC), BF16),
        scratch_shapes=[pltpu.VMEM((tn, k), BF16), pltpu.VMEM((CARRY + tm, tn), F32)],
        compiler_params=_params(("arbitrary", "arbitrary")),
        name="ssd_bc_proj",
    )(a, w_in_t, conv_w, conv_b.reshape(1, SSD_CONV_DIM))


def _stack3(x, place_ref):
    parts = _split3(x)
    out = _dot(parts[0], place_ref[0]) + _dot(parts[1], place_ref[1]) + _dot(parts[2], place_ref[2])
    return out.astype(BF16)


def _dt_kernel(a_ref, w_ref, bias_ref, alog_ref, tri_ref, place_ref, dt_ref, cum_ref):
    dt = _softplus(_dot_nt(a_ref[...], w_ref[...].astype(BF16)) + bias_ref[...])
    da = dt * (-jnp.exp(alog_ref[...]))
    cum = _dot01(tri_ref[...], da, sel_first=True)
    dt_ref[...] = _stack3(dt, place_ref)
    cum_ref[...] = _stack3(cum, place_ref)


def _dt_path(a, w_in_t, layer, dt_bias, a_log):
    rows, k = a.shape
    dt_row_blk = (SSD_DI + SSD_CONV_DIM) // SSD_HEADS
    assert dt_row_blk * SSD_HEADS == SSD_DI + SSD_CONV_DIM
    r = np.arange(ROW_TILE)
    tri = ((r[:, None] // CHUNK == r[None, :] // CHUNK) & (r[:, None] >= r[None, :]))
    tri = jnp.asarray(tri, BF16)
    place = np.zeros((3, SSD_HEADS, SPLIT_LANES), np.float32)
    for term in range(3):
        place[term, np.arange(SSD_HEADS), term * SSD_HEADS + np.arange(SSD_HEADS)] = 1.0
    place = jnp.asarray(place, BF16)
    shp = jax.ShapeDtypeStruct((rows, SPLIT_LANES), BF16)
    return pl.pallas_call(
        _dt_kernel,
        grid=(rows // ROW_TILE,),
        in_specs=[pl.BlockSpec((ROW_TILE, k), lambda i: (i, 0)),
                  pl.BlockSpec((None, SSD_HEADS, k), lambda i: (layer, dt_row_blk, 0)),
                  pl.BlockSpec((1, SSD_HEADS), lambda i: (0, 0)),
                  pl.BlockSpec((1, SSD_HEADS), lambda i: (0, 0)),
                  pl.BlockSpec((ROW_TILE, ROW_TILE), lambda i: (0, 0)),
                  pl.BlockSpec((3, SSD_HEADS, SPLIT_LANES), lambda i: (0, 0, 0))],
        out_specs=[pl.BlockSpec((ROW_TILE, SPLIT_LANES), lambda i: (i, 0)),
                   pl.BlockSpec((ROW_TILE, SPLIT_LANES), lambda i: (i, 0))],
        out_shape=[shp, shp],
        compiler_params=_params(("arbitrary",)),
        name="ssd_dt_path",
    )(a, w_in_t, dt_bias.reshape(1, SSD_HEADS), a_log.reshape(1, SSD_HEADS), tri, place)


def _out_kernel(a_ref, w_ref, h_ref, m_ref, nw_ref, *refs, n_load, keep_residual):
    if keep_residual:
        ho_ref, hn_ref, wb_ref = refs
    else:
        hn_ref, wb_ref = refs
    step = pl.program_id(0)

    @pl.when(step < n_load)
    def _():
        r0 = pl.multiple_of(step * W_CHUNK, W_CHUNK)
        wb_ref[pl.ds(r0, W_CHUNK), :] = w_ref[...].astype(BF16)

    @pl.when(step >= n_load)
    def _():
        hnew = (h_ref[...] + _dot(a_ref[...], wb_ref[...])) * m_ref[...]
        if keep_residual:
            ho_ref[...] = hnew
        hn_ref[...] = _rms_scale(hnew, nw_ref[...]).astype(hn_ref.dtype)


def _out_row_tile(k):
    return 416 if k <= 4096 else 320


def _out_proj(a, w, layer, h, vmask, norm_w, hn_dtype, name, keep_residual=True):
    rows, k = a.shape
    d = w.shape[2]
    tm = _out_row_tile(k)
    n_load = k // W_CHUNK
    assert n_load * W_CHUNK == k and rows % tm == 0

    def row(step):
        return jnp.maximum(step - n_load, 0)

    row_blk = pl.BlockSpec((tm, d), lambda s: (row(s), 0))
    hn_shape = jax.ShapeDtypeStruct((rows, d), hn_dtype)
    if keep_residual:
        out_specs, out_shape = [row_blk, row_blk], [jax.ShapeDtypeStruct((rows, d), F32), hn_shape]
    else:
        out_specs, out_shape = [row_blk], [hn_shape]
    return pl.pallas_call(
        functools.partial(_out_kernel, n_load=n_load, keep_residual=keep_residual),
        grid=(n_load + rows // tm,),
        in_specs=[pl.BlockSpec((tm, k), lambda s: (row(s), 0)),
                  pl.BlockSpec((None, W_CHUNK, d), lambda s: (layer, jnp.minimum(s, n_load - 1), 0)),
                  row_blk,
                  pl.BlockSpec((tm, 1), lambda s: (row(s), 0)),
                  pl.BlockSpec((1, d), lambda s: (0, 0))],
        out_specs=out_specs,
        out_shape=out_shape,
        scratch_shapes=[pltpu.VMEM((k, d), BF16)],
        compiler_params=_params(("arbitrary",)),
        name=name,
    )(a, w, h, vmask, norm_w.reshape(1, d))


def _ret_scan_kernel(q_ref, k_ref, v_ref, gs_ref, gnw_ref, intra_ref, qd_ref, kd_ref, cd_ref,
                     y_ref, state_ref, *, chunks):
    @pl.when(pl.program_id(2) == 0)
    def _():
        state_ref[...] = jnp.zeros(state_ref.shape, F32)

    intra = intra_ref[0]
    q_decay = qd_ref[0]
    k_decay = kd_ref[0]
    chunk_decay = cd_ref[0]
    gnw = gnw_ref[...]

    def body(c, carry):
        r0 = c * RET_CHUNK
        q = q_ref[pl.ds(r0, RET_CHUNK), :]
        k = k_ref[pl.ds(r0, RET_CHUNK), :]
        v = v_ref[pl.ds(r0, RET_CHUNK), :]
        qb = q.astype(BF16)
        scores = _dot_nt(qb, k.astype(BF16)) * intra
        o = _dot(scores.astype(BF16), v)
        state = state_ref[...]
        o = o + q_decay * _dot(qb, state.astype(BF16))
        kd = (k * k_decay).astype(BF16)
        state_ref[...] = state * chunk_decay + _dot_tn(kd, v)
        o = o * lax.rsqrt(jnp.mean(o * o, axis=-1, keepdims=True) + NORM_EPS) * gnw
        y_ref[pl.ds(r0, RET_CHUNK), :] = (gs_ref[pl.ds(r0, RET_CHUNK), :] * o).astype(y_ref.dtype)
        return carry

    for c in range(chunks):
        body(c, 0)


def _ret_scan(qk, v, gs, gn_w, bsz, length):
    rows = qk.shape[0]
    log_gamma = np.log1p(-np.exp2(-5.0 - np.arange(RET_HEADS, dtype=np.float64)))
    chunk = RET_CHUNK
    tm = RET_ROW_TILE
    assert tm % chunk == 0 and length % tm == 0
    idx = np.arange(chunk, dtype=np.float64)
    diff = idx[:, None] - idx[None, :]
    intra = np.where(diff[None] >= 0, np.exp(np.maximum(diff, 0.0)[None] * log_gamma[:, None, None]), 0.0)
    q_decay = np.exp((idx + 1.0)[None, :] * log_gamma[:, None])[..., None]
    k_decay = np.exp((chunk - 1.0 - idx)[None, :] * log_gamma[:, None])[..., None]
    chunk_decay = np.exp(chunk * log_gamma)[:, None, None]
    intra, q_decay, k_decay, chunk_decay = (jnp.asarray(t, F32) for t in (intra, q_decay, k_decay, chunk_decay))
    tpb = length // tm
    nkh = RET_QK // RET_DK
    return pl.pallas_call(
        functools.partial(_ret_scan_kernel, chunks=tm // chunk),
        grid=(bsz, RET_HEADS, tpb),
        in_specs=[pl.BlockSpec((tm, RET_DK), lambda b, h, t: (b * tpb + t, h)),
                  pl.BlockSpec((tm, RET_DK), lambda b, h, t: (b * tpb + t, nkh + h)),
                  pl.BlockSpec((tm, RET_DV), lambda b, h, t: (b * tpb + t, h)),
                  pl.BlockSpec((tm, RET_DV), lambda b, h, t: (b * tpb + t, h)),
                  pl.BlockSpec((1, RET_DV), lambda b, h, t: (0, h)),
                  pl.BlockSpec((1, chunk, chunk), lambda b, h, t: (h, 0, 0)),
                  pl.BlockSpec((1, chunk, 1), lambda b, h, t: (h, 0, 0)),
                  pl.BlockSpec((1, chunk, 1), lambda b, h, t: (h, 0, 0)),
                  pl.BlockSpec((1, 1, 1), lambda b, h, t: (h, 0, 0))],
        out_specs=pl.BlockSpec((tm, RET_DV), lambda b, h, t: (b * tpb + t, h)),
        out_shape=jax.ShapeDtypeStruct((rows, RET_VDIM), BF16),
        scratch_shapes=[pltpu.VMEM((RET_DK, RET_DV), F32)],
        compiler_params=_params(("arbitrary", "arbitrary", "arbitrary")),
        name="ret_scan",
    )(qk, qk, v, gs, gn_w.reshape(1, RET_VDIM), intra, q_decay, k_decay, chunk_decay)


def _ssd_scan_kernel(x_ref, b_ref, c_ref, z_ref, dt_ref, cum_ref, sel_ref, d_ref, gw_ref,
                     y_ref, state_ref, dte_ref, cume_ref, *, chunks):
    @pl.when(pl.program_id(2) == 0)
    def _():
        state_ref[...] = jnp.zeros(state_ref.shape, F32)

    sel = sel_ref[...]
    dte_ref[...] = _dot(dt_ref[...], sel)
    cume_ref[...] = _dot(cum_ref[...], sel)

    gwid = SSD_GW
    row = lax.broadcasted_iota(jnp.int32, (CHUNK, gwid), 0)
    lane_pos = lax.broadcasted_iota(jnp.int32, (CHUNK, gwid), 1) & (CHUNK - 1)
    causal = row >= lane_pos
    diag = row == lane_pos
    quad = 4 * SSD_HEADDIM
    blk_r = lax.broadcasted_iota(jnp.int32, (quad, quad), 0) // SSD_HEADDIM
    blk_c = lax.broadcasted_iota(jnp.int32, (quad, quad), 1) // SSD_HEADDIM
    blockdiag = blk_r == blk_c
    d_skip = d_ref[...]
    gw = gw_ref[...]

    def body(c, carry):
        r0 = c * CHUNK
        rows = pl.ds(r0, CHUNK)
        x = x_ref[rows, :]
        cum = cume_ref[rows, :]
        bb = b_ref[rows, :].astype(BF16)
        cb = c_ref[rows, :].astype(BF16)
        xdt = x * dte_ref[rows, :]
        cum_row = jnp.sum(jnp.where(diag, cum, 0.0), axis=0, keepdims=True)
        decay = jnp.where(causal, jnp.exp(cum - cum_row), 0.0)
        gram = _dot_nt(cb, jnp.concatenate([bb] * SSD_HPG, axis=0))
        attn = (gram * decay).astype(BF16)
        xdt_b = xdt.astype(BF16)
        ys = []
        for qd in range(gwid // quad):
            xq = xdt_b[:, qd * quad:(qd + 1) * quad]
            rhs = jnp.where(blockdiag, jnp.concatenate([xq] * 4, axis=0), jnp.zeros((), BF16))
            ys.append(_dot(attn[:, qd * quad:(qd + 1) * quad], rhs))
        y = jnp.concatenate(ys, axis=1)
        state = state_ref[...]
        y = y + _dot(cb, state.astype(BF16)) * jnp.exp(cum)
        cum_last = cum[CHUNK - 1:CHUNK, :]
        xw = (xdt * jnp.exp(cum_last - cum)).astype(BF16)
        state_ref[...] = state * jnp.exp(cum_last) + _dot_tn(bb, xw)
        y = (y + x * d_skip) * z_ref[rows, :]
        y = y * lax.rsqrt(jnp.mean(y * y, axis=-1, keepdims=True) + NORM_EPS) * gw
        y_ref[rows, :] = y.astype(y_ref.dtype)
        return carry

    for c in range(chunks):
        body(c, 0)


def _ssd_scan(xs, bc, zs, dt, cum, d_skip, gnorm_w, bsz, tiles_per_batch):
    rows = xs.shape[0]
    tpb = tiles_per_batch
    b_blk = 0
    c_blk = SSD_BC // SSD_STATE
    d_exp = jnp.repeat(d_skip, SSD_HEADDIM).reshape(1, SSD_DI)
    one_hot = np.arange(SSD_HEADS)[:, None] == (np.arange(SSD_DI)[None, :] // SSD_HEADDIM)
    expand = np.zeros((SPLIT_LANES, SSD_DI), np.float32)
    expand[:3 * SSD_HEADS] = np.tile(one_hot, (3, 1))
    expand = jnp.asarray(expand, BF16)
    wide = pl.BlockSpec((ROW_TILE, SSD_GW), lambda b, g, t: (b * tpb + t, g))
    heads = pl.BlockSpec((ROW_TILE, SPLIT_LANES), lambda b, g, t: (b * tpb + t, 0))
    return pl.pallas_call(
        functools.partial(_ssd_scan_kernel, chunks=ROW_TILE // CHUNK),
        grid=(bsz, SSD_GROUPS, tpb),
        in_specs=[wide,
                  pl.BlockSpec((ROW_TILE, SSD_STATE), lambda b, g, t: (b * tpb + t, b_blk + g)),
                  pl.BlockSpec((ROW_TILE, SSD_STATE), lambda b, g, t: (b * tpb + t, c_blk + g)),
                  wide, heads, heads,
                  pl.BlockSpec((SPLIT_LANES, SSD_GW), lambda b, g, t: (0, g)),
                  pl.BlockSpec((1, SSD_GW), lambda b, g, t: (0, g)),
                  pl.BlockSpec((1, SSD_GW), lambda b, g, t: (0, g))],
        out_specs=wide,
        out_shape=jax.ShapeDtypeStruct((rows, SSD_DI), BF16),
        scratch_shapes=[pltpu.VMEM((SSD_STATE, SSD_GW), F32),
                        pltpu.VMEM((ROW_TILE, SSD_GW), F32), pltpu.VMEM((ROW_TILE, SSD_GW), F32)],
        compiler_params=_params(("arbitrary", "arbitrary", "arbitrary")),
        name="ssd_scan",
    )(xs, bc, bc, zs, dt, cum, expand, d_exp, gnorm_w.reshape(1, SSD_DI))


def kernel(x, meta_tokens, ret_norm_w, ret_w_in, ret_gn_w, ret_w_out, ssd_norm_w, ssd_w_in, ssd_conv_w, ssd_conv_b, ssd_dt_bias, ssd_a_log, ssd_d, ssd_gnorm_w, ssd_w_out, ffn_norm_w, ffn_w_up, ffn_conv_w, ffn_conv_b, ffn_w_down, final_norm_w):
    bsz, seq, d = x.shape
    length = META_PAD + N_META + seq
    assert d == D_MODEL and length % ROW_TILE == 0 and ROW_TILE % CHUNK == 0 and length % PROJ_ROW_TILE == 0
    tpb = length // ROW_TILE
    depth = ffn_w_up.shape[0]

    meta = jnp.broadcast_to(meta_tokens.astype(x.dtype)[None], (bsz, N_META, d))
    h = jnp.concatenate([jnp.zeros((bsz, META_PAD, d), x.dtype), meta, x], axis=1).reshape(bsz * length, d)

    pos_i = np.arange(length) - META_PAD
    vmask = jnp.asarray(np.tile((pos_i >= 0).astype(np.float32), bsz).reshape(bsz * length, 1))
    half = RET_DK // 2
    inv = ROPE_BASE ** (-jnp.arange(half, dtype=F32) / half)
    ang = jnp.asarray(pos_i, F32)[:, None] * inv[None, :]
    cos, sin = jnp.cos(ang), jnp.sin(ang)

    ssd_w_in_t = jnp.swapaxes(ssd_w_in, 1, 2)
    hn = _rmsnorm(h, ret_norm_w[0])
    for i in range(depth):
        j = i // 2
        if i % 2 == 0:
            qk = _qk_proj(hn, ret_w_in, j, cos, sin, length // PROJ_ROW_TILE)
            v = _proj(hn, ret_w_in, j, 2 * RET_QK, RET_VDIM, BF16, name="ret_v_proj")
            gs = _proj(hn, ret_w_in, j, 2 * RET_QK + RET_VDIM, RET_VDIM, BF16, act="silu", name="ret_g_proj")
            y = _ret_scan(qk, v, gs, ret_gn_w[j], bsz, length)
            h, hn = _out_proj(y, ret_w_out, j, h, vmask, ffn_norm_w[i], BF16, name="ret_out_proj")
        else:
            xs, zs = _xz_proj(hn, ssd_w_in_t, j, ssd_conv_w[j], ssd_conv_b[j], vmask, length)
            bc = _bc_proj(hn, ssd_w_in_t, j, ssd_conv_w[j], ssd_conv_b[j], length)
            dt, cum = _dt_path(hn, ssd_w_in_t, j, ssd_dt_bias[j], ssd_a_log[j])
            y = _ssd_scan(xs, bc, zs, dt, cum, ssd_d[j], ssd_gnorm_w[j], bsz, tpb)
            h, hn = _out_proj(y, ssd_w_out, j, h, vmask, ffn_norm_w[i], BF16, name="ssd_out_proj")
        act = _ffn_up(hn, ffn_w_up, i, ffn_conv_w[i], ffn_conv_b[i], length)
        if i == depth - 1:
            (hn,) = _out_proj(act, ffn_w_down, i, h, vmask, final_norm_w, x.dtype,
                              name="ffn_down_final", keep_residual=False)
        else:
            next_w = ssd_norm_w[(i + 1) // 2] if (i + 1) % 2 == 1 else ret_norm_w[(i + 1) // 2]
            h, hn = _out_proj(act, ffn_w_down, i, h, vmask, next_w, BF16, name="ffn_down_proj")
    return hn.reshape(bsz, length, d)[:, META_PAD + N_META:]
```
